```python
import math
import jax
import jax.numpy as jnp
from jax import lax
import numpy as np

D_MODEL = 2048
BATCH = 4
SEQ = 4096
DEPTH = 2

HEAD_DIM = 128
ROPE_THETA = 10000.0
ATTN_Q_BLOCK = 128
NEG = -1e30
LN_EPS = 1e-5
DN_ALPHA = (2 * DEPTH) ** 0.25
DN_BETA = (8 * DEPTH) ** -0.25

DIFF_HEADS = D_MODEL // (2 * HEAD_DIM)
DIFF_QK_DIM = HEAD_DIM // 2
MOBA_HEADS = D_MODEL // (2 * HEAD_DIM)
MOBA_BLOCK = 256
MOBA_TOPK = 3
MOBA_Q_CHUNK = 32
NSA_HEADS = D_MODEL // HEAD_DIM
NSA_KV_GROUPS = 2
NSA_CMP_LEN = 32
NSA_CMP_STRIDE = 16
NSA_SEL_BLOCK = 64
NSA_SEL_N = 16
NSA_WINDOW = 512
NSA_Q_CHUNK = 32
NSA_FORCED = 1e4
N_EXPERTS = 64
EXPERT_DIM = 512
TOP_K = 8
N_GROUPS = 8
TOPK_GROUPS = 4
ROUTED_SCALE = 2.5
MOE_CHUNK = 512

MIX_WIDTH = (DIFF_HEADS + MOBA_HEADS) * HEAD_DIM
L0_SPLITS = (DIFF_HEADS * 2 * DIFF_QK_DIM, DIFF_HEADS * 2 * DIFF_QK_DIM, DIFF_HEADS * HEAD_DIM,
             MOBA_HEADS * HEAD_DIM, MOBA_HEADS * HEAD_DIM, MOBA_HEADS * HEAD_DIM)
L1_SPLITS = (NSA_HEADS * HEAD_DIM,) + (NSA_KV_GROUPS * HEAD_DIM,) * 6 + (NSA_HEADS * 3,)

kernel_name = 'hybrid_diff_moba_nsa_moe_deepnorm'


def _split(t, sizes):
    return jnp.split(t, np.cumsum(sizes)[:-1].tolist(), axis=-1)


def layernorm(x, g, b):
    xf = x.astype(jnp.float32)
    mu = jnp.mean(xf, axis=-1, keepdims=True)
    var = jnp.mean(jnp.square(xf - mu), axis=-1, keepdims=True)
    return ((xf - mu) * lax.rsqrt(var + LN_EPS) * g + b).astype(x.dtype)


def rope(x, pos):
    d = x.shape[-1]
    half = d // 2
    inv = ROPE_THETA ** (-jnp.arange(half, dtype=jnp.float32) * 2.0 / d)
    ang = pos.astype(jnp.float32)[:, None] * inv[None, :]
    cos, sin = jnp.cos(ang), jnp.sin(ang)
    xf = x.astype(jnp.float32)
    x1, x2 = xf[..., :half], xf[..., half:]
    return jnp.concatenate([x1 * cos - x2 * sin, x2 * cos + x1 * sin], axis=-1).astype(x.dtype)


def diff_attention_group(q, k, v, lam_q1, lam_k1, lam_q2, lam_k2, subln_g, pos, layer_idx):
    B, S, _ = q.shape
    H, dq = DIFF_HEADS, DIFF_QK_DIM
    f32 = jnp.float32
    q = rope(q.reshape(B, S, H, 2, dq).transpose(3, 0, 2, 1, 4), pos)
    k = rope(k.reshape(B, S, H, 2, dq).transpose(3, 0, 2, 1, 4), pos)
    v = v.reshape(B, S, H, HEAD_DIM).transpose(0, 2, 1, 3)
    lam_init = 0.8 - 0.6 * math.exp(-0.3 * layer_idx)
    lam = (jnp.exp(jnp.sum(lam_q1.astype(f32) * lam_k1.astype(f32)))
           - jnp.exp(jnp.sum(lam_q2.astype(f32) * lam_k2.astype(f32))) + lam_init)
    scale = dq ** -0.5
    kpos = jnp.arange(S)

    def q_block(i):
        st = i * ATTN_Q_BLOCK
        qb = lax.dynamic_slice_in_dim(q, st, ATTN_Q_BLOCK, axis=3)
        qpos = st + jnp.arange(ATTN_Q_BLOCK)
        mask = kpos[None, :] <= qpos[:, None]
        s = jnp.einsum('mbhqd,mbhkd->mbhqk', qb, k).astype(f32) * scale
        p = jax.nn.softmax(jnp.where(mask, s, NEG), axis=-1)
        a = p[0] - lam * p[1]
        return jnp.einsum('bhqk,bhkd->bhqd', a.astype(v.dtype), v)

    o = lax.map(q_block, jnp.arange(S // ATTN_Q_BLOCK))
    o = o.transpose(1, 2, 0, 3, 4).reshape(B, H, S, HEAD_DIM).astype(f32)
    o = o * lax.rsqrt(jnp.mean(jnp.square(o), axis=-1, keepdims=True) + LN_EPS) * subln_g * (1.0 - lam_init)
    return o.astype(v.dtype).transpose(0, 2, 1, 3).reshape(B, S, H * HEAD_DIM)


def moba_group(q, k, v, pos):
    B, S, _ = q.shape
    H, D, BLK, C = MOBA_HEADS, HEAD_DIM, MOBA_BLOCK, MOBA_Q_CHUNK
    f32 = jnp.float32
    heads = lambda t: t.reshape(B, S, H, D).transpose(0, 2, 1, 3)
    q, k, v = rope(heads(q), pos), rope(heads(k), pos), heads(v)
    nb = -(-S // BLK)
    pad = nb * BLK - S
    padw = ((0, 0), (0, 0), (0, pad), (0, 0))
    q, k, v = jnp.pad(q, padw), jnp.pad(k, padw), jnp.pad(v, padw)
    kb = k.reshape(B, H, nb, BLK, D)
    vb = v.reshape(B, H, nb, BLK, D)
    kmean = jnp.mean(kb.astype(f32), axis=3)
    n_pick = max(1, min(MOBA_TOPK, nb - 1))
    n_sel_keys = n_pick * BLK
    scale = D ** -0.5
    bi = jnp.arange(B)[:, None, None, None]
    hi = jnp.arange(H)[None, :, None, None]
    blk_ids = jnp.arange(nb)

    def q_chunk(c):
        st = c * C
        qc = lax.dynamic_slice_in_dim(q, st, C, axis=2)
        qpos = st + jnp.arange(C)
        own = st // BLK
        gate = jnp.einsum('bhqd,bhnd->bhqn', qc.astype(f32), kmean)
        gate = jnp.where(blk_ids < own, gate, NEG)
        _, idx = lax.top_k(gate, n_pick)
        picked = jnp.repeat(idx < own, BLK, axis=-1)
        k_sel = kb[bi, hi, idx].reshape(B, H, C, n_sel_keys, D)
        v_sel = vb[bi, hi, idx].reshape(B, H, C, n_sel_keys, D)
        k_own = lax.dynamic_index_in_dim(kb, own, axis=2, keepdims=False)
        v_own = lax.dynamic_index_in_dim(vb, own, axis=2, keepdims=False)
        own_mask = (own * BLK + jnp.arange(BLK))[None, :] <= qpos[:, None]
        s_sel = jnp.einsum('bhqd,bhqkd->bhqk', qc, k_sel).astype(f32) * scale
        s_own = jnp.einsum('bhqd,bhkd->bhqk', qc, k_own).astype(f32) * scale
        s = jnp.concatenate([jnp.where(picked, s_sel, NEG), jnp.where(own_mask, s_own, NEG)], axis=-1)
        p = jax.nn.softmax(s, axis=-1).astype(v.dtype)
        return (jnp.einsum('bhqk,bhqkd->bhqd', p[..., :n_sel_keys], v_sel)
                + jnp.einsum('bhqk,bhkd->bhqd', p[..., n_sel_keys:], v_own))

    o = lax.map(q_chunk, jnp.arange(nb * BLK // C))
    o = o.transpose(1, 2, 0, 3, 4).reshape(B, H, nb * BLK, D)[:, :, :S]
    return o.transpose(0, 2, 1, 3).reshape(B, S, H * D)


def diff_moba_mixer(h, w_in, lam_q1, lam_k1, lam_q2, lam_k2, subln_g, w_out, layer_idx):
    S = h.shape[1]
    pos = jnp.arange(S)
    dq, dk, dv, mq, mk, mv = _split(h @ w_in, L0_SPLITS)
    o_a = diff_attention_group(dq, dk, dv, lam_q1, lam_k1, lam_q2, lam_k2, subln_g, pos, layer_idx)
    o_b = moba_group(mq, mk, mv, pos)
    return jnp.concatenate([o_a, o_b], axis=-1) @ w_out


def nsa_mixer(h, w_in, pe_k, w1_k, w2_k, pe_v, w1_v, w2_v, w_out):
    B, S, _ = h.shape
    H, G, D = NSA_HEADS, NSA_KV_GROUPS, HEAD_DIM
    R = H // G
    f32 = jnp.float32
    q, kc, vc, ks, vs, kw, vw, gl = _split(h @ w_in, L1_SPLITS)
    pos = jnp.arange(S)
    scale = D ** -0.5
    q = q.reshape(B, S, G, R, D).transpose(0, 2, 3, 1, 4)
    q_rot = rope(q, pos)
    kv_heads = lambda t: t.reshape(B, S, G, D).transpose(0, 2, 1, 3)

    n_cmp = (S - NSA_CMP_LEN) // NSA_CMP_STRIDE + 1
    cmp_start = jnp.arange(n_cmp) * NSA_CMP_STRIDE
    gather_idx = cmp_start[:, None] + jnp.arange(NSA_CMP_LEN)[None, :]

    def compress(t, pe, w1, w2):
        blocks = kv_heads(t)[:, :, gather_idx] + pe
        hid = jax.nn.gelu(blocks.reshape(B, G, n_cmp, NSA_CMP_LEN * D) @ w1)
        return hid @ w2

    k_cmp = compress(kc, pe_k, w1_k, w2_k)
    v_cmp = compress(vc, pe_v, w1_v, w2_v)
    cmp_mask = (cmp_start + NSA_CMP_LEN - 1)[None, :] <= pos[:, None]
    s_c = jnp.einsum('bgrsd,bgnd->bgrsn', q, k_cmp).astype(f32) * scale
    p_c = jax.nn.softmax(jnp.where(cmp_mask, s_c, NEG), axis=-1) * cmp_mask
    o_cmp = jnp.einsum('bgrsn,bgnd->bgrsd', p_c.astype(v_cmp.dtype), v_cmp).astype(f32)

    n_blk = S // NSA_SEL_BLOCK
    blk_start = jnp.arange(n_blk) * NSA_SEL_BLOCK
    overlap = jnp.clip(jnp.minimum(cmp_start[:, None] + NSA_CMP_LEN, blk_start[None, :] + NSA_SEL_BLOCK)
                       - jnp.maximum(cmp_start[:, None], blk_start[None, :]), 0).astype(f32) / NSA_CMP_LEN
    imp = jnp.einsum('bgrsn,nj->bgsj', p_c, overlap)
    cur = pos // NSA_SEL_BLOCK
    blk = jnp.arange(n_blk)[None, :]
    allowed = blk <= cur[:, None]
    forced = (blk == 0) | (blk == cur[:, None]) | (blk == cur[:, None] - 1)
    imp = jnp.where(allowed, jnp.where(forced, NSA_FORCED, imp), NEG)
    n_pick = min(NSA_SEL_N, n_blk)
    _, sel_idx = lax.top_k(imp, n_pick)
    k_slc = rope(kv_heads(ks), pos).reshape(B, G, n_blk, NSA_SEL_BLOCK, D)
    v_slc = kv_heads(vs).reshape(B, G, n_blk, NSA_SEL_BLOCK, D)
    n_keys = n_pick * NSA_SEL_BLOCK
    bi = jnp.arange(B)[:, None, None, None]
    gi = jnp.arange(G)[None, :, None, None]

    def sel_chunk(c):
        st = c * NSA_Q_CHUNK
        qc = lax.dynamic_slice_in_dim(q_rot, st, NSA_Q_CHUNK, axis=3)
        idx = lax.dynamic_slice_in_dim(sel_idx, st, NSA_Q_CHUNK, axis=2)
        kg = k_slc[bi, gi, idx].reshape(B, G, NSA_Q_CHUNK, n_keys, D)
        vg = v_slc[bi, gi, idx].reshape(B, G, NSA_Q_CHUNK, n_keys, D)
        qpos = st + jnp.arange(NSA_Q_CHUNK)
        kpos = (idx[..., None] * NSA_SEL_BLOCK + jnp.arange(NSA_SEL_BLOCK)).reshape(B, G, NSA_Q_CHUNK, n_keys)
        m = kpos <= qpos[None, None, :, None]
        s = jnp.einsum('bgrqd,bgqkd->bgrqk', qc, kg).astype(f32) * scale
        p = jax.nn.softmax(jnp.where(m[:, :, None], s, NEG), axis=-1)
        return jnp.einsum('bgrqk,bgqkd->bgrqd', p.astype(vg.dtype), vg)

    o_sel = lax.map(sel_chunk, jnp.arange(S // NSA_Q_CHUNK))
    o_sel = o_sel.transpose(1, 2, 3, 0, 4, 5).reshape(B, G, R, S, D).astype(f32)

    padw = ((0, 0), (0, 0), (NSA_WINDOW, 0), (0, 0))
    k_win = jnp.pad(rope(kv_heads(kw), pos), padw)
    v_win = jnp.pad(kv_heads(vw), padw)
    band = NSA_WINDOW + ATTN_Q_BLOCK

    def win_block(i):
        st = i * ATTN_Q_BLOCK
        qb = lax.dynamic_slice_in_dim(q_rot, st, ATTN_Q_BLOCK, axis=3)
        kb = lax.dynamic_slice_in_dim(k_win, st, band, axis=2)
        vb = lax.dynamic_slice_in_dim(v_win, st, band, axis=2)
        kpos = st - NSA_WINDOW + jnp.arange(band)
        qpos = st + jnp.arange(ATTN_Q_BLOCK)
        m = ((kpos[None, :] <= qpos[:, None]) & (kpos[None, :] > qpos[:, None] - NSA_WINDOW)
             & (kpos[None, :] >= 0))
        s = jnp.einsum('bgrqd,bgkd->bgrqk', qb, kb).astype(f32) * scale
        p = jax.nn.softmax(jnp.where(m, s, NEG), axis=-1)
        return jnp.einsum('bgrqk,bgkd->bgrqd', p.astype(vb.dtype), vb)

    o_win = lax.map(win_block, jnp.arange(S // ATTN_Q_BLOCK))
    o_win = o_win.transpose(1, 2, 3, 0, 4, 5).reshape(B, G, R, S, D).astype(f32)

    g = jax.nn.sigmoid(gl.astype(f32)).reshape(B, S, G, R, 3).transpose(0, 2, 3, 1, 4)
    o = (g[..., 0:1] * o_cmp + g[..., 1:2] * o_sel + g[..., 2:3] * o_win).astype(h.dtype)
    return o.transpose(0, 3, 1, 2, 4).reshape(B, S, H * D) @ w_out


def moe(h, router_w, router_b, w_gu, w_down, ws_gu, ws_down):
    B, S, Dm = h.shape
    T = B * S
    f32 = jnp.float32
    xt = h.reshape(T, Dm)
    score = jax.nn.sigmoid((xt @ router_w).astype(f32))
    biased = score + router_b.astype(f32)
    grp_score = lax.top_k(biased.reshape(T, N_GROUPS, N_EXPERTS // N_GROUPS), 2)[0].sum(-1)
    _, gidx = lax.top_k(grp_score, TOPK_GROUPS)
    gmask = jnp.any(gidx[:, :, None] == jnp.arange(N_GROUPS), axis=1)
    emask = jnp.repeat(gmask, N_EXPERTS // N_GROUPS, axis=1)
    _, eidx = lax.top_k(jnp.where(emask, biased, NEG), TOP_K)
    w = jnp.take_along_axis(score, eidx, axis=1)
    w = w / jnp.sum(w, axis=-1, keepdims=True) * ROUTED_SCALE

    n_slots = T * TOP_K
    flat_e = eidx.reshape(-1)
    flat_t = jnp.repeat(jnp.arange(T, dtype=jnp.int32), TOP_K)
    flat_w = w.reshape(-1)
    order = jnp.argsort(flat_e)
    se = flat_e[order]
    counts = jnp.bincount(flat_e, length=N_EXPERTS)
    pcounts = (counts + MOE_CHUNK - 1) // MOE_CHUNK * MOE_CHUNK
    pend = jnp.cumsum(pcounts)
    pstart = pend - pcounts
    cstart = jnp.cumsum(counts) - counts
    dest = pstart[se] + (jnp.arange(n_slots) - cstart[se])
    n_rows = (-(-n_slots // MOE_CHUNK)) * MOE_CHUNK + N_EXPERTS * MOE_CHUNK
    n_chunks = n_rows // MOE_CHUNK
    buf_t = jnp.zeros((n_rows,), jnp.int32).at[dest].set(flat_t[order])
    buf_w = jnp.zeros((n_rows,), xt.dtype).at[dest].set(flat_w[order].astype(xt.dtype))
    chunk_e = jnp.clip(jnp.searchsorted(pend, jnp.arange(n_chunks) * MOE_CHUNK, side='right'), 0, N_EXPERTS - 1)

    def expert_chunk(acc, inp):
        tok, wt, e = inp
        gate, up = jnp.split(xt[tok] @ w_gu[e], 2, axis=-1)
        y = (jax.nn.silu(gate) * up) @ w_down[e]
        return acc.at[tok].add(y * wt[:, None]), None

    routed, _ = lax.scan(expert_chunk, jnp.zeros_like(xt),
                         (buf_t.reshape(n_chunks, MOE_CHUNK), buf_w.reshape(n_chunks, MOE_CHUNK), chunk_e))
    sg, su = jnp.split(xt @ ws_gu, 2, axis=-1)
    shared = (jax.nn.silu(sg) * su) @ ws_down
    return (routed + shared).reshape(B, S, Dm)


def setup_inputs(seed: int = 0) -> dict:
    key = jax.random.key(seed)
    k = jax.random.split(key, 32)
    nrm = lambda i, shape, s: jax.random.normal(k[i], shape, jnp.float32) * s
    NE = (DEPTH + 1) // 2
    NO = DEPTH // 2
    D = D_MODEL
    L0_IN = sum(L0_SPLITS)
    L1_IN = sum(L1_SPLITS)
    CMP_IN = NSA_CMP_LEN * HEAD_DIM
    return {
        'x': nrm(0, (BATCH, SEQ, D), 1.0),
        'a_w_in': nrm(1, (NE, D, L0_IN), D ** -0.5),
        'a_lam_q1': nrm(2, (NE, DIFF_QK_DIM), 0.1),
        'a_lam_k1': nrm(3, (NE, DIFF_QK_DIM), 0.1),
        'a_lam_q2': nrm(4, (NE, DIFF_QK_DIM), 0.1),
        'a_lam_k2': nrm(5, (NE, DIFF_QK_DIM), 0.1),
        'a_subln_g': 1.0 + nrm(6, (NE, HEAD_DIM), 0.02),
        'a_w_out': nrm(7, (NE, MIX_WIDTH, D), MIX_WIDTH ** -0.5 * DN_BETA),
        'c_w_in': nrm(8, (NO, D, L1_IN), D ** -0.5),
        'c_pe_k': nrm(9, (NO, NSA_CMP_LEN, HEAD_DIM), 0.02),
        'c_w1_k': nrm(10, (NO, CMP_IN, HEAD_DIM), CMP_IN ** -0.5),
        'c_w2_k': nrm(11, (NO, HEAD_DIM, HEAD_DIM), HEAD_DIM ** -0.5),
        'c_pe_v': nrm(12, (NO, NSA_CMP_LEN, HEAD_DIM), 0.02),
        'c_w1_v': nrm(13, (NO, CMP_IN, HEAD_DIM), CMP_IN ** -0.5),
        'c_w2_v': nrm(14, (NO, HEAD_DIM, HEAD_DIM), HEAD_DIM ** -0.5),
        'c_w_out': nrm(15, (NO, NSA_HEADS * HEAD_DIM, D), (NSA_HEADS * HEAD_DIM) ** -0.5 * DN_BETA),
        'ln_mix_g': 1.0 + nrm(16, (DEPTH, D), 0.02),
        'ln_mix_b': nrm(17, (DEPTH, D), 0.02),
        'ln_ffn_g': 1.0 + nrm(18, (DEPTH, D), 0.02),
        'ln_ffn_b': nrm(19, (DEPTH, D), 0.02),
        'router_w': nrm(20, (DEPTH, D, N_EXPERTS), D ** -0.5),
        'router_b': nrm(21, (DEPTH, N_EXPERTS), 0.01),
        'w_gu': nrm(22, (DEPTH, N_EXPERTS, D, 2 * EXPERT_DIM), D ** -0.5),
        'w_down': nrm(23, (DEPTH, N_EXPERTS, EXPERT_DIM, D), EXPERT_DIM ** -0.5 * DN_BETA),
        'ws_gu': nrm(24, (DEPTH, D, 2 * EXPERT_DIM), D ** -0.5),
        'ws_down': nrm(25, (DEPTH, EXPERT_DIM, D), EXPERT_DIM ** -0.5 * DN_BETA),
    }


def reference(x, a_w_in, a_lam_q1, a_lam_k1, a_lam_q2, a_lam_k2, a_subln_g, a_w_out,
              c_w_in, c_pe_k, c_w1_k, c_w2_k, c_pe_v, c_w1_v, c_w2_v, c_w_out,
              ln_mix_g, ln_mix_b, ln_ffn_g, ln_ffn_b,
              router_w, router_b, w_gu, w_down, ws_gu, ws_down):
    h = x
    for i in range(DEPTH):
        j = i // 2
        if i % 2 == 0:
            mix = diff_moba_mixer(h, a_w_in[j], a_lam_q1[j], a_lam_k1[j], a_lam_q2[j], a_lam_k2[j],
                                  a_subln_g[j], a_w_out[j], i)
        else:
            mix = nsa_mixer(h, c_w_in[j], c_pe_k[j], c_w1_k[j], c_w2_k[j], c_pe_v[j], c_w1_v[j], c_w2_v[j],
                            c_w_out[j])
        h = layernorm(DN_ALPHA * h + mix, ln_mix_g[i], ln_mix_b[i])
        ffn = moe(h, router_w[i], router_b[i], w_gu[i], w_down[i], ws_gu[i], ws_down[i])
        h = layernorm(DN_ALPHA * h + ffn, ln_ffn_g[i], ln_ffn_b[i])
    return h
```

```python
import functools
import math

import jax
import jax.numpy as jnp
import numpy as np
from jax import lax
from jax.experimental import pallas as pl
from jax.experimental.pallas import tpu as pltpu

F32 = jnp.float32
BF16 = jnp.bfloat16
I32 = jnp.int32

HEAD_DIM = 128
ROPE_THETA = 10000.0
NEG = -1e30
LN_EPS = 1e-5
DEPTH = 2
DN_ALPHA = (2 * DEPTH) ** 0.25

DIFF_HEADS = 8
MOBA_HEADS = 8
MOBA_BLOCK = 256
MOBA_TOPK = 3
NSA_HEADS = 16
NSA_GROUPS = 2
NSA_REP = NSA_HEADS // NSA_GROUPS
NSA_CMP_LEN = 32
NSA_CMP_STRIDE = 16
NSA_SEL_BLOCK = 64
NSA_SEL_N = 16
NSA_WINDOW = 512
NSA_FORCED = 1e4
N_EXPERTS = 64
EXPERT_DIM = 512
TOP_K = 8
N_GROUPS = 8
GROUP_SIZE = N_EXPERTS // N_GROUPS
TOPK_GROUPS = 4
ROUTED_SCALE = 2.5

LANES = 128
VMEM_LIMIT = 56 * 1024 * 1024

_NT = (((1,), (1,)), ((), ()))


def _cp(sem, **kw):
    return pltpu.CompilerParams(dimension_semantics=sem, vmem_limit_bytes=VMEM_LIMIT, **kw)


def _layernorm(x, g, b):
    mu = jnp.mean(x, axis=-1, keepdims=True)
    xc = x - mu
    var = jnp.mean(xc * xc, axis=-1, keepdims=True)
    return xc * lax.rsqrt(var + LN_EPS) * g + b


def _mm_kernel(x_ref, w_ref, o_ref):
    o_ref[...] = jnp.dot(x_ref[...].astype(BF16), w_ref[...],
                         preferred_element_type=F32).astype(o_ref.dtype)


def _matmul(x, w, tm, tn, out_dtype=F32):
    m, k = x.shape
    n = w.shape[1]
    tm, tn = min(tm, m), min(tn, n)
    return pl.pallas_call(
        _mm_kernel,
        grid=(m // tm, n // tn),
        in_specs=[pl.BlockSpec((tm, k), lambda i, j: (i, 0)),
                  pl.BlockSpec((k, tn), lambda i, j: (0, j))],
        out_specs=pl.BlockSpec((tm, tn), lambda i, j: (i, j)),
        out_shape=jax.ShapeDtypeStruct((m, n), out_dtype),
        compiler_params=_cp(("parallel", "arbitrary")),
        name="proj_in",
    )(x, w)


def _proj_ln_kernel(*refs, n_in, alpha):
    a_refs, w_refs = refs[:n_in], refs[n_in:2 * n_in]
    h_ref, g_ref, b_ref, o_ref = refs[2 * n_in:]
    acc = alpha * h_ref[...]
    for a_ref, w_ref in zip(a_refs, w_refs):
        acc = acc + jnp.dot(a_ref[...], w_ref[...], preferred_element_type=F32)
    o_ref[...] = _layernorm(acc, g_ref[...], b_ref[...])


def _proj_ln(acts, ws, h, g, b, tm=256):
    m, d = h.shape
    tm = min(tm, m)
    n_in = len(acts)
    in_specs = ([pl.BlockSpec((tm, a.shape[1]), lambda i: (i, 0)) for a in acts]
                + [pl.BlockSpec(w.shape, lambda i: (0, 0)) for w in ws]
                + [pl.BlockSpec((tm, d), lambda i: (i, 0)),
                   pl.BlockSpec((1, d), lambda i: (0, 0)),
                   pl.BlockSpec((1, d), lambda i: (0, 0))])
    return pl.pallas_call(
        functools.partial(_proj_ln_kernel, n_in=n_in, alpha=DN_ALPHA),
        grid=(m // tm,),
        in_specs=in_specs,
        out_specs=pl.BlockSpec((tm, d), lambda i: (i, 0)),
        out_shape=jax.ShapeDtypeStruct((m, d), F32),
        compiler_params=_cp(("parallel",)),
        name="proj_out_ln",
    )(*acts, *ws, h, g.reshape(1, d), b.reshape(1, d))


def _rope_tables(seq, d):
    half = d // 2
    inv = ROPE_THETA ** (-jnp.arange(half, dtype=F32) * 2.0 / d)
    ang = jnp.arange(seq, dtype=F32)[:, None] * inv[None, :]
    cos, sin = jnp.cos(ang), jnp.sin(ang)
    return jnp.concatenate([cos, cos], axis=1), jnp.concatenate([-sin, sin], axis=1)


def _rope128(x, cos, sin_signed):
    return x * cos + pltpu.roll(x, 64, 1) * sin_signed


def _rope64x2(x, cos, sin_lo, sin_hi):
    return x * cos + pltpu.roll(x, 96, 1) * sin_lo + pltpu.roll(x, 32, 1) * sin_hi


def _split0_kernel(p_ref, c64_ref, slo_ref, shi_ref, c128_ref, s128_ref,
                   dq_ref, dk_ref, dv_ref, mq_ref, mk_ref, mv_ref, km_ref, *, rows):
    c64, slo, shi = c64_ref[...], slo_ref[...], shi_ref[...]
    c128, s128 = c128_ref[...], s128_ref[...]
    w = DIFF_HEADS * HEAD_DIM
    dscale = (HEAD_DIM // 2) ** -0.5
    mscale = HEAD_DIM ** -0.5
    for h in range(DIFF_HEADS):
        lo, hi = h * HEAD_DIM, (h + 1) * HEAD_DIM
        dq_ref[:, lo:hi] = (_rope64x2(p_ref[:, lo:hi], c64, slo, shi) * dscale).astype(BF16)
        dk_ref[:, lo:hi] = _rope64x2(p_ref[:, w + lo:w + hi], c64, slo, shi).astype(BF16)
        dv_ref[:, lo:hi] = p_ref[:, 2 * w + lo:2 * w + hi].astype(BF16)
        mq_ref[:, lo:hi] = (_rope128(p_ref[:, 3 * w + lo:3 * w + hi], c128, s128) * mscale).astype(BF16)
        kr = _rope128(p_ref[:, 4 * w + lo:4 * w + hi], c128, s128)
        mk_ref[:, lo:hi] = kr.astype(BF16)
        km_ref[0, :, lo:hi] = jnp.sum(kr, axis=0, keepdims=True) * (1.0 / rows)
        mv_ref[:, lo:hi] = p_ref[:, 5 * w + lo:5 * w + hi].astype(BF16)


def _split0(p, seq):
    t = p.shape[0]
    rows = MOBA_BLOCK
    w = DIFF_HEADS * HEAD_DIM
    nb = seq // rows
    c64h, s64h = _rope_tables(seq, HEAD_DIM // 2)
    c64 = jnp.concatenate([c64h, c64h], axis=1)
    lane = jnp.arange(HEAD_DIM) % (HEAD_DIM // 2)
    s64 = jnp.concatenate([s64h, s64h], axis=1)
    slo = jnp.where(lane < HEAD_DIM // 4, s64, 0.0)
    shi = jnp.where(lane >= HEAD_DIM // 4, s64, 0.0)
    c128, s128 = _rope_tables(seq, HEAD_DIM)
    tab = pl.BlockSpec((rows, HEAD_DIM), lambda i: (i % nb, 0))
    out = pl.BlockSpec((rows, w), lambda i: (i, 0))
    bf = jax.ShapeDtypeStruct((t, w), BF16)
    return pl.pallas_call(
        functools.partial(_split0_kernel, rows=rows),
        grid=(t // rows,),
        in_specs=[pl.BlockSpec((rows, 6 * w), lambda i: (i, 0)), tab, tab, tab, tab, tab],
        out_specs=[out] * 6 + [pl.BlockSpec((1, 1, w), lambda i: (i, 0, 0))],
        out_shape=[bf] * 6 + [jax.ShapeDtypeStruct((t // rows, 1, w), F32)],
        compiler_params=_cp(("parallel",)),
        name="split_rope_l0",
    )(p, c64, slo, shi, c128, s128)


def _flash_update(carry, s, v):
    m, l, acc = carry
    m_new = jnp.maximum(m, jnp.max(s, axis=1, keepdims=True))
    p = jnp.exp(s - m_new)
    a = jnp.exp(m - m_new)
    l = a * l + jnp.sum(p, axis=1, keepdims=True)
    acc = a * acc + jnp.dot(p.astype(BF16), v, preferred_element_type=F32)
    return m_new, l, acc


def _flash_init(rows, d):
    return (jnp.full((rows, 1), NEG, F32), jnp.zeros((rows, 1), F32), jnp.zeros((rows, d), F32))


def _diff_kernel(lam_ref, q_ref, k_ref, v_ref, g_ref, o_ref, *, tq, out_scale):
    i = pl.program_id(2)
    q = q_ref[0]
    lane = lax.broadcasted_iota(I32, (tq, HEAD_DIM), 1)
    zero = jnp.zeros_like(q)
    qs = jnp.concatenate([jnp.where(lane < HEAD_DIM // 2, q, zero),
                          jnp.where(lane >= HEAD_DIM // 2, q, zero)], axis=0)

    def tile(j):
        st = pl.multiple_of(j * tq, tq)
        k = k_ref[0, pl.ds(st, tq), :]
        v = v_ref[0, pl.ds(st, tq), :]
        return lax.dot_general(qs, k, _NT, preferred_element_type=F32), v

    def body(j, carry):
        s, v = tile(j)
        return _flash_update(carry, s, v)

    carry = lax.fori_loop(0, i, body, _flash_init(2 * tq, HEAD_DIM))
    s, v = tile(i)
    r = lax.broadcasted_iota(I32, (2 * tq, tq), 0)
    c = lax.broadcasted_iota(I32, (2 * tq, tq), 1)
    s = jnp.where(c <= jnp.where(r >= tq, r - tq, r), s, NEG)
    _, l, acc = _flash_update(carry, s, v)
    o = acc / l
    od = o[:tq] - lam_ref[0] * o[tq:]
    od = od * lax.rsqrt(jnp.mean(od * od, axis=1, keepdims=True) + LN_EPS) * g_ref[...] * out_scale
    o_ref[0] = od.astype(o_ref.dtype)


def _diff_attention(lam, q, k, v, subln_g, lam_init, tq=256):
    b, s, _ = q.shape
    tq = min(tq, s)
    kv = pl.BlockSpec((1, s, HEAD_DIM), lambda bi, h, i, *_: (bi, 0, h))
    qo = pl.BlockSpec((1, tq, HEAD_DIM), lambda bi, h, i, *_: (bi, i, h))
    return pl.pallas_call(
        functools.partial(_diff_kernel, tq=tq, out_scale=1.0 - lam_init),
        grid_spec=pltpu.PrefetchScalarGridSpec(
            num_scalar_prefetch=1,
            grid=(b, DIFF_HEADS, s // tq),
            in_specs=[qo, kv, kv, pl.BlockSpec((1, HEAD_DIM), lambda bi, h, i, *_: (0, 0))],
            out_specs=qo),
        out_shape=jax.ShapeDtypeStruct(q.shape, BF16),
        compiler_params=_cp(("parallel", "parallel", "arbitrary")),
        name="diff_attention",
    )(lam, q, k, v, subln_g.reshape(1, HEAD_DIM))


def _moba_kernel(q_ref, k_ref, v_ref, km_ref, o_ref, *, blk, nb, n_pick):
    i = pl.program_id(2)
    q = q_ref[0]
    gate = lax.dot_general(q.astype(F32), km_ref[0], _NT, precision=lax.Precision.HIGHEST,
                           preferred_element_type=F32)
    bid = lax.broadcasted_iota(I32, (blk, nb), 1)
    past = bid < i
    g = jnp.where(past, gate, NEG)
    sel = jnp.zeros((blk, nb), F32)
    for _ in range(n_pick):
        mx = jnp.max(g, axis=1, keepdims=True)
        idx = jnp.min(jnp.where(g == mx, bid, nb), axis=1, keepdims=True)
        pick = bid == idx
        sel = jnp.where(pick, jnp.where(past, 1.0, 0.0), sel)
        g = jnp.where(pick, -jnp.inf, g)

    def tile(j):
        st = pl.multiple_of(j * blk, blk)
        k = k_ref[0, pl.ds(st, blk), :]
        v = v_ref[0, pl.ds(st, blk), :]
        return lax.dot_general(q, k, _NT, preferred_element_type=F32), v

    s, v = tile(i)
    r = lax.broadcasted_iota(I32, (blk, blk), 0)
    c = lax.broadcasted_iota(I32, (blk, blk), 1)
    carry = _flash_update(_flash_init(blk, HEAD_DIM), jnp.where(c <= r, s, NEG), v)

    def body(j, carry):
        s, v = tile(j)
        on = jnp.sum(jnp.where(bid == j, sel, 0.0), axis=1, keepdims=True) > 0.5
        return _flash_update(carry, jnp.where(on, s, NEG), v)

    _, l, acc = lax.fori_loop(0, i, body, carry)
    o_ref[0] = (acc / l).astype(o_ref.dtype)


def _moba_attention(q, k, v, kmean):
    b, s, _ = q.shape
    blk = min(MOBA_BLOCK, s)
    nb = s // blk
    n_pick = max(1, min(MOBA_TOPK, nb - 1))
    kv = pl.BlockSpec((1, s, HEAD_DIM), lambda bi, h, i: (bi, 0, h))
    qo = pl.BlockSpec((1, blk, HEAD_DIM), lambda bi, h, i: (bi, i, h))
    return pl.pallas_call(
        functools.partial(_moba_kernel, blk=blk, nb=nb, n_pick=n_pick),
        grid=(b, MOBA_HEADS, nb),
        in_specs=[qo, kv, kv, pl.BlockSpec((1, nb, HEAD_DIM), lambda bi, h, i: (bi, 0, h))],
        out_specs=qo,
        out_shape=jax.ShapeDtypeStruct(q.shape, BF16),
        compiler_params=_cp(("parallel", "parallel", "arbitrary")),
        name="moba_attention",
    )(q, k, v, kmean)


def _split1_kernel(p_ref, c_ref, s_ref, q_ref, qr_ref, ks_ref, vs_ref, kw_ref, vw_ref, gt_ref):
    cos, sin = c_ref[...], s_ref[...]
    scale = HEAD_DIM ** -0.5
    wq = NSA_HEADS * HEAD_DIM
    wkv = NSA_GROUPS * HEAD_DIM
    for h in range(NSA_HEADS):
        lo, hi = h * HEAD_DIM, (h + 1) * HEAD_DIM
        x = p_ref[:, lo:hi]
        q_ref[:, lo:hi] = (x * scale).astype(BF16)
        qr_ref[:, lo:hi] = (_rope128(x, cos, sin) * scale).astype(BF16)
    base = wq + 2 * wkv
    for g in range(NSA_GROUPS):
        lo, hi = g * HEAD_DIM, (g + 1) * HEAD_DIM
        ks_ref[:, lo:hi] = _rope128(p_ref[:, base + lo:base + hi], cos, sin).astype(BF16)
        vs_ref[:, lo:hi] = p_ref[:, base + wkv + lo:base + wkv + hi].astype(BF16)
        kw_ref[:, lo:hi] = _rope128(p_ref[:, base + 2 * wkv + lo:base + 2 * wkv + hi], cos, sin).astype(BF16)
        vw_ref[:, lo:hi] = p_ref[:, base + 3 * wkv + lo:base + 3 * wkv + hi].astype(BF16)
    gt_ref[...] = jax.nn.sigmoid(p_ref[:, base + 4 * wkv:base + 4 * wkv + LANES])


def _split1(p, seq, rows=256):
    t, wp = p.shape
    rows = min(rows, seq)
    nb = seq // rows
    wq = NSA_HEADS * HEAD_DIM
    wkv = NSA_GROUPS * HEAD_DIM
    cos, sin = _rope_tables(seq, HEAD_DIM)
    tab = pl.BlockSpec((rows, HEAD_DIM), lambda i: (i % nb, 0))
    oq = pl.BlockSpec((rows, wq), lambda i: (i, 0))
    okv = pl.BlockSpec((rows, wkv), lambda i: (i, 0))
    sq = jax.ShapeDtypeStruct((t, wq), BF16)
    skv = jax.ShapeDtypeStruct((t, wkv), BF16)
    return pl.pallas_call(
        _split1_kernel,
        grid=(t // rows,),
        in_specs=[pl.BlockSpec((rows, wp), lambda i: (i, 0)), tab, tab],
        out_specs=[oq, oq, okv, okv, okv, okv, pl.BlockSpec((rows, LANES), lambda i: (i, 0))],
        out_shape=[sq, sq, skv, skv, skv, skv, jax.ShapeDtypeStruct((t, LANES), F32)],
        compiler_params=_cp(("parallel",)),
        name="split_rope_l1",
    )(p, cos, sin)


def _gelu_tanh(x):
    return 0.5 * x * (1.0 + jnp.tanh(math.sqrt(2.0 / math.pi) * (x + 0.044715 * (x * x * x))))


def _compress_kernel(c_ref, pelo_ref, pehi_ref, w1lo_ref, w1hi_ref, w2_ref, o_ref, *, nc):
    c = c_ref[0]
    a = jnp.dot((c + pelo_ref[...]).astype(BF16), w1lo_ref[...], preferred_element_type=F32)
    bm = jnp.dot((c + pehi_ref[...]).astype(BF16), w1hi_ref[...], preferred_element_type=F32)
    hid = _gelu_tanh(a + pltpu.roll(bm, nc - 1, 0))
    o_ref[0] = jnp.dot(hid.astype(BF16), w2_ref[...], preferred_element_type=F32).astype(o_ref.dtype)


def _compress(chunks, pe, w1, w2):
    bg, nc, cw = chunks.shape
    half = NSA_CMP_STRIDE
    pelo = pe[:half].reshape(1, cw)
    pehi = pe[half:].reshape(1, cw)
    w1lo = w1[:cw].astype(BF16)
    w1hi = w1[cw:].astype(BF16)
    full = lambda a: pl.BlockSpec(a.shape, lambda i: (0,) * a.ndim)
    return pl.pallas_call(
        functools.partial(_compress_kernel, nc=nc),
        grid=(bg,),
        in_specs=[pl.BlockSpec((1, nc, cw), lambda i: (i, 0, 0)), full(pelo), full(pehi),
                  full(w1lo), full(w1hi), pl.BlockSpec(w2.shape, lambda i: (0, 0))],
        out_specs=pl.BlockSpec((1, nc, HEAD_DIM), lambda i: (i, 0, 0)),
        out_shape=jax.ShapeDtypeStruct((bg, nc, HEAD_DIM), BF16),
        compiler_params=_cp(("parallel",)),
        name="nsa_compress",
    )(chunks, pelo, pehi, w1lo, w1hi, w2.astype(BF16))


def _stack_heads(q):
    return jnp.concatenate([q[:, r * HEAD_DIM:(r + 1) * HEAD_DIM] for r in range(NSA_REP)], axis=0)


def _unstack_store(o_ref, o, tq):
    for r in range(NSA_REP):
        o_ref[0, :, r * HEAD_DIM:(r + 1) * HEAD_DIM] = o[r * tq:(r + 1) * tq].astype(o_ref.dtype)


def _cmp_select_kernel(q_ref, kc_ref, vc_ref, o_ref, sel_ref, *, tq, nc, nblk, n_pick):
    i = pl.program_id(2)
    qs = _stack_heads(q_ref[0])
    s = lax.dot_general(qs, kc_ref[0], _NT, preferred_element_type=F32)
    rows = NSA_REP * tq
    row = lax.broadcasted_iota(I32, (rows, nc), 0)
    n = lax.broadcasted_iota(I32, (rows, nc), 1)
    pos = i * tq + (row & (tq - 1))
    valid = n * NSA_CMP_STRIDE + (NSA_CMP_LEN - 1) <= pos
    sm = jnp.where(valid, s, NEG)
    e = jnp.exp(sm - jnp.max(sm, axis=1, keepdims=True))
    p = jnp.where(valid, e / jnp.sum(e, axis=1, keepdims=True), 0.0)
    o = jnp.dot(p.astype(BF16), vc_ref[0], preferred_element_type=F32)
    _unstack_store(o_ref, o, tq)

    psum = p[0:tq]
    for r in range(1, NSA_REP):
        psum = psum + p[r * tq:(r + 1) * tq]
    cn = lax.broadcasted_iota(I32, (nc, nblk), 0) * NSA_CMP_STRIDE
    bj = lax.broadcasted_iota(I32, (nc, nblk), 1) * NSA_SEL_BLOCK
    ov = jnp.maximum(jnp.minimum(cn + NSA_CMP_LEN, bj + NSA_SEL_BLOCK) - jnp.maximum(cn, bj), 0)
    ov = ov.astype(F32) * (1.0 / NSA_CMP_LEN)
    imp = jnp.dot(psum, ov, precision=lax.Precision.HIGHEST, preferred_element_type=F32)

    cur = (i * tq + lax.broadcasted_iota(I32, (tq, nblk), 0)) // NSA_SEL_BLOCK
    blk = lax.broadcasted_iota(I32, (tq, nblk), 1)
    allowed = blk <= cur
    forced = (blk == 0) | (blk == cur) | (blk == cur - 1)
    val = jnp.where(allowed, jnp.where(forced, NSA_FORCED, imp), NEG)
    sel = jnp.zeros((tq, nblk), F32)
    for _ in range(n_pick):
        mx = jnp.max(val, axis=1, keepdims=True)
        idx = jnp.min(jnp.where(val == mx, blk, nblk), axis=1, keepdims=True)
        pick = blk == idx
        sel = jnp.where(pick, 1.0, sel)
        val = jnp.where(pick, -jnp.inf, val)
    sel_ref[0, 0] = jnp.where(allowed, sel, 0.0)


def _cmp_select(q, kcmp, vcmp, tq=128):
    b, s, _ = q.shape
    tq = min(tq, s)
    nc = kcmp.shape[1]
    nblk = s // NSA_SEL_BLOCK
    n_pick = min(NSA_SEL_N, nblk)
    wg = NSA_REP * HEAD_DIM
    qo = pl.BlockSpec((1, tq, wg), lambda bi, g, i: (bi, i, g))
    kv = pl.BlockSpec((1, nc, HEAD_DIM), lambda bi, g, i: (bi * NSA_GROUPS + g, 0, 0))
    return pl.pallas_call(
        functools.partial(_cmp_select_kernel, tq=tq, nc=nc, nblk=nblk, n_pick=n_pick),
        grid=(b, NSA_GROUPS, s // tq),
        in_specs=[qo, kv, kv],
        out_specs=[qo, pl.BlockSpec((1, 1, tq, nblk), lambda bi, g, i: (bi, g, i, 0))],
        out_shape=[jax.ShapeDtypeStruct(q.shape, F32),
                   jax.ShapeDtypeStruct((b, NSA_GROUPS, s, nblk), F32)],
        compiler_params=_cp(("parallel", "parallel", "arbitrary")),
        name="nsa_cmp_select",
    )(q, kcmp, vcmp)


def _mask_heads(s, keep, tq):
    tk = s.shape[1]
    s3 = jnp.where(keep[None], s.reshape(NSA_REP, tq, tk), NEG)
    return s3.reshape(NSA_REP * tq, tk)


def _sel_kernel(q_ref, k_ref, v_ref, sel_ref, o_ref, *, tq, tk, nblk):
    i = pl.program_id(2)
    qs = _stack_heads(q_ref[0])
    sel = sel_ref[0, 0].astype(BF16)
    per = tk // NSA_SEL_BLOCK
    qpos = i * tq + lax.broadcasted_iota(I32, (tq, tk), 0)
    kcol = lax.broadcasted_iota(I32, (tq, tk), 1)
    eb = lax.broadcasted_iota(I32, (nblk, tk), 0)
    ec = lax.broadcasted_iota(I32, (nblk, tk), 1) // NSA_SEL_BLOCK

    def body(j, carry):
        st = pl.multiple_of(j * tk, tk)
        k = k_ref[0, pl.ds(st, tk), :]
        v = v_ref[0, pl.ds(st, tk), :]
        s = lax.dot_general(qs, k, _NT, preferred_element_type=F32)
        expand = jnp.where(eb == j * per + ec, 1.0, 0.0).astype(BF16)
        on = jnp.dot(sel, expand, preferred_element_type=F32)
        keep = jnp.where(j * tk + kcol <= qpos, on, 0.0) > 0.5
        return _flash_update(carry, _mask_heads(s, keep, tq), v)

    n_tiles = (i * tq + tq - 1) // tk + 1
    _, l, acc = lax.fori_loop(0, n_tiles, body, _flash_init(NSA_REP * tq, HEAD_DIM))
    _unstack_store(o_ref, acc / l, tq)


def _sel_attention(q, k, v, sel, tq=128, tk=512):
    b, s, _ = q.shape
    tq, tk = min(tq, s), min(tk, s)
    nblk = s // NSA_SEL_BLOCK
    wg = NSA_REP * HEAD_DIM
    qo = pl.BlockSpec((1, tq, wg), lambda bi, g, i: (bi, i, g))
    kv = pl.BlockSpec((1, s, HEAD_DIM), lambda bi, g, i: (bi, 0, g))
    return pl.pallas_call(
        functools.partial(_sel_kernel, tq=tq, tk=tk, nblk=nblk),
        grid=(b, NSA_GROUPS, s // tq),
        in_specs=[qo, kv, kv, pl.BlockSpec((1, 1, tq, nblk), lambda bi, g, i: (bi, g, i, 0))],
        out_specs=qo,
        out_shape=jax.ShapeDtypeStruct(q.shape, F32),
        compiler_params=_cp(("parallel", "parallel", "arbitrary")),
        name="nsa_sel_attention",
    )(q, k, v, sel)


def _win_kernel(q_ref, k_ref, v_ref, oc_ref, os_ref, gt_ref, o_ref, *, tq, window):
    i = pl.program_id(2)
    qs = _stack_heads(q_ref[0])
    qpos = i * tq + lax.broadcasted_iota(I32, (tq, tq), 0)
    kcol = lax.broadcasted_iota(I32, (tq, tq), 1)

    def body(j, carry):
        st = pl.multiple_of(j * tq, tq)
        k = k_ref[0, pl.ds(st, tq), :]
        v = v_ref[0, pl.ds(st, tq), :]
        s = lax.dot_general(qs, k, _NT, preferred_element_type=F32)
        kpos = j * tq + kcol
        keep = (kpos <= qpos) & (kpos > qpos - window)
        return _flash_update(carry, _mask_heads(s, keep, tq), v)

    def rev_body(t, carry):
        return body(i - t, carry)

    n_tiles = jnp.minimum(i, window // tq) + 1
    _, l, acc = lax.fori_loop(0, n_tiles, rev_body, _flash_init(NSA_REP * tq, HEAD_DIM))
    ow = acc / l
    gt = gt_ref[0]
    for r in range(NSA_REP):
        lo, hi = r * HEAD_DIM, (r + 1) * HEAD_DIM
        mix = (gt[:, 3 * r:3 * r + 1] * oc_ref[0, :, lo:hi]
               + gt[:, 3 * r + 1:3 * r + 2] * os_ref[0, :, lo:hi]
               + gt[:, 3 * r + 2:3 * r + 3] * ow[r * tq:(r + 1) * tq])
        o_ref[0, :, lo:hi] = mix.astype(o_ref.dtype)


def _win_merge(q, k, v, o_cmp, o_sel, gates, tq=128):
    b, s, _ = q.shape
    tq = min(tq, s)
    wg = NSA_REP * HEAD_DIM
    qo = pl.BlockSpec((1, tq, wg), lambda bi, g, i: (bi, i, g))
    kv = pl.BlockSpec((1, s, HEAD_DIM), lambda bi, g, i: (bi, 0, g))
    return pl.pallas_call(
        functools.partial(_win_kernel, tq=tq, window=NSA_WINDOW),
        grid=(b, NSA_GROUPS, s // tq),
        in_specs=[qo, kv, kv, qo, qo, pl.BlockSpec((1, tq, LANES), lambda bi, g, i: (bi, i, g))],
        out_specs=qo,
        out_shape=jax.ShapeDtypeStruct(q.shape, BF16),
        compiler_params=_cp(("parallel", "parallel", "arbitrary")),
        name="nsa_window_merge",
    )(q, k, v, o_cmp, o_sel, gates)


def _router_kernel(x_ref, w_ref, b_ref, idx_ref, wt_ref):
    tm = x_ref.shape[0]
    logits = jnp.dot(x_ref[...], w_ref[...], precision=lax.Precision.HIGHEST,
                     preferred_element_type=F32)
    score = jax.nn.sigmoid(logits)
    biased = score + b_ref[...]
    lane = lax.broadcasted_iota(I32, (tm, N_EXPERTS), 1)
    grp = lane // GROUP_SIZE

    gscore = jnp.zeros((tm, N_EXPERTS), F32)
    for g in range(N_GROUPS):
        ing = grp == g
        v = jnp.where(ing, biased, -jnp.inf)
        m1 = jnp.max(v, axis=1, keepdims=True)
        i1 = jnp.min(jnp.where(v == m1, lane, N_EXPERTS), axis=1, keepdims=True)
        m2 = jnp.max(jnp.where(lane == i1, -jnp.inf, v), axis=1, keepdims=True)
        gscore = jnp.where(ing, m1 + m2, gscore)

    emask = jnp.zeros((tm, N_EXPERTS), F32)
    for _ in range(TOPK_GROUPS):
        mx = jnp.max(gscore, axis=1, keepdims=True)
        gi = jnp.min(jnp.where(gscore == mx, grp, N_GROUPS), axis=1, keepdims=True)
        pick = grp == gi
        emask = jnp.where(pick, 1.0, emask)
        gscore = jnp.where(pick, -jnp.inf, gscore)

    val = jnp.where(emask > 0.5, biased, NEG)
    kcol = lax.broadcasted_iota(I32, (tm, TOP_K), 1)
    idx_out = jnp.zeros((tm, TOP_K), I32)
    wt_out = jnp.zeros((tm, TOP_K), F32)
    for kk in range(TOP_K):
        mx = jnp.max(val, axis=1, keepdims=True)
        ei = jnp.min(jnp.where(val == mx, lane, N_EXPERTS), axis=1, keepdims=True)
        pick = lane == ei
        wk = jnp.sum(jnp.where(pick, score, 0.0), axis=1, keepdims=True)
        idx_out = jnp.where(kcol == kk, ei, idx_out)
        wt_out = jnp.where(kcol == kk, wk, wt_out)
        val = jnp.where(pick, -jnp.inf, val)
    idx_ref[...] = idx_out
    wt_ref[...] = wt_out / jnp.sum(wt_out, axis=1, keepdims=True) * ROUTED_SCALE


def _router(x, w, bias, tm=256):
    t, d = x.shape
    tm = min(tm, t)
    return pl.pallas_call(
        _router_kernel,
        grid=(t // tm,),
        in_specs=[pl.BlockSpec((tm, d), lambda i: (i, 0)),
                  pl.BlockSpec((d, N_EXPERTS), lambda i: (0, 0)),
                  pl.BlockSpec((1, N_EXPERTS), lambda i: (0, 0))],
        out_specs=[pl.BlockSpec((tm, TOP_K), lambda i: (i, 0)),
                   pl.BlockSpec((tm, TOP_K), lambda i: (i, 0))],
        out_shape=[jax.ShapeDtypeStruct((t, TOP_K), I32), jax.ShapeDtypeStruct((t, TOP_K), F32)],
        compiler_params=_cp(("parallel",)),
        name="moe_router",
    )(x, w, bias.reshape(1, N_EXPERTS))


def _moe_plan(eidx, tm):
    t = eidx.shape[0]
    n_slots = t * TOP_K
    flat_e = eidx.reshape(-1)
    order = jnp.argsort(flat_e, stable=True).astype(I32)
    se = flat_e[order]
    counts = jnp.zeros((N_EXPERTS,), I32).at[flat_e].add(1)
    pcounts = (counts + tm - 1) // tm * tm
    pend = jnp.cumsum(pcounts)
    pstart = pend - pcounts
    cstart = jnp.cumsum(counts) - counts
    dest = (pstart[se] + (jnp.arange(n_slots, dtype=I32) - cstart[se])).astype(I32)
    n_rows = n_slots + N_EXPERTS * tm
    n_tiles = n_rows // tm
    row_tok = jnp.zeros((n_rows,), I32).at[dest].set(order // TOP_K)
    slot_row = jnp.zeros((n_slots,), I32).at[order].set(dest)
    tile_e = jnp.clip(jnp.searchsorted(pend, jnp.arange(n_tiles, dtype=I32) * tm, side="right"),
                      0, N_EXPERTS - 1).astype(I32)
    n_used = (pend[-1] // tm).astype(I32).reshape(1)
    return row_tok, slot_row, tile_e, n_used, n_tiles


def _gather_kernel(nused_ref, idx_hbm, x_hbm, o_hbm, idx_smem, idx_sem, row_sem, *, rb):
    i = pl.program_id(0)

    @pl.when(i < nused_ref[0])
    def _():
        cp = pltpu.make_async_copy(idx_hbm.at[i], idx_smem, idx_sem)
        cp.start()
        cp.wait()
        base = i * rb

        def issue(r, c):
            pltpu.make_async_copy(x_hbm.at[pl.ds(idx_smem[r], 1)], o_hbm.at[pl.ds(base + r, 1)],
                                  row_sem).start()
            return c

        lax.fori_loop(0, rb, issue, 0, unroll=8)
        pltpu.make_async_copy(x_hbm.at[pl.ds(0, rb)], o_hbm.at[pl.ds(base, rb)], row_sem).wait()

    @pl.when(i >= nused_ref[0])
    def _():
        fill = pltpu.make_async_copy(x_hbm.at[pl.ds(0, rb)], o_hbm.at[pl.ds(i * rb, rb)], row_sem)
        fill.start()
        fill.wait()


def _gather_rows(x, row_tok, n_used_steps, rb):
    n_rows = row_tok.shape[0]
    d = x.shape[1]
    steps = n_rows // rb
    return pl.pallas_call(
        functools.partial(_gather_kernel, rb=rb),
        grid_spec=pltpu.PrefetchScalarGridSpec(
            num_scalar_prefetch=1,
            grid=(steps,),
            in_specs=[pl.BlockSpec(memory_space=pl.ANY), pl.BlockSpec(memory_space=pl.ANY)],
            out_specs=pl.BlockSpec(memory_space=pl.ANY),
            scratch_shapes=[pltpu.SMEM((rb,), I32), pltpu.SemaphoreType.DMA(()),
                            pltpu.SemaphoreType.DMA(())]),
        out_shape=jax.ShapeDtypeStruct((n_rows, d), x.dtype),
        compiler_params=_cp(("arbitrary",)),
        name="moe_dispatch_gather",
    )(n_used_steps, row_tok.reshape(steps, rb), x)


def _expert_kernel(te_ref, nused_ref, x_ref, wgu_ref, wd_ref, o_ref):
    i = pl.program_id(0)

    @pl.when(i < nused_ref[0])
    def _():
        gu = jnp.dot(x_ref[...].astype(BF16), wgu_ref[0], preferred_element_type=F32)
        act = jax.nn.silu(gu[:, :EXPERT_DIM]) * gu[:, EXPERT_DIM:]
        o_ref[...] = jnp.dot(act.astype(BF16), wd_ref[0], preferred_element_type=F32)

    @pl.when(i >= nused_ref[0])
    def _():
        o_ref[...] = jnp.zeros_like(o_ref)


def _expert_ffn(xs, w_gu, w_down, tile_e, n_used, tm):
    n_rows, d = xs.shape
    n_tiles = n_rows // tm
    last = lambda i, te, nu: jnp.minimum(i, nu[0] - 1)
    return pl.pallas_call(
        _expert_kernel,
        grid_spec=pltpu.PrefetchScalarGridSpec(
            num_scalar_prefetch=2,
            grid=(n_tiles,),
            in_specs=[pl.BlockSpec((tm, d), lambda i, te, nu: (last(i, te, nu), 0)),
                      pl.BlockSpec((1, d, 2 * EXPERT_DIM), lambda i, te, nu: (te[last(i, te, nu)], 0, 0)),
                      pl.BlockSpec((1, EXPERT_DIM, d), lambda i, te, nu: (te[last(i, te, nu)], 0, 0))],
            out_specs=pl.BlockSpec((tm, d), lambda i, te, nu: (i, 0))),
        out_shape=jax.ShapeDtypeStruct((n_rows, d), F32),
        compiler_params=_cp(("arbitrary",)),
        name="moe_expert_ffn",
    )(tile_e, n_used, xs, w_gu, w_down)


def _shared_kernel(x_ref, wgu_ref, wd_ref, o_ref):
    gu = jnp.dot(x_ref[...].astype(BF16), wgu_ref[...], preferred_element_type=F32)
    act = jax.nn.silu(gu[:, :EXPERT_DIM]) * gu[:, EXPERT_DIM:]
    o_ref[...] = jnp.dot(act.astype(BF16), wd_ref[...], preferred_element_type=F32)


def _shared_ffn(x, w_gu, w_down, tm=512):
    t, d = x.shape
    tm = min(tm, t)
    return pl.pallas_call(
        _shared_kernel,
        grid=(t // tm,),
        in_specs=[pl.BlockSpec((tm, d), lambda i: (i, 0)),
                  pl.BlockSpec(w_gu.shape, lambda i: (0, 0)),
                  pl.BlockSpec(w_down.shape, lambda i: (0, 0))],
        out_specs=pl.BlockSpec((tm, d), lambda i: (i, 0)),
        out_shape=jax.ShapeDtypeStruct((t, d), F32),
        compiler_params=_cp(("parallel",)),
        name="moe_shared_ffn",
    )(x, w_gu, w_down)


def _combine_kernel(rows_hbm, ys_hbm, wt_ref, h_ref, sh_ref, g_ref, b_ref, o_ref,
                    idx_smem, buf, idx_sem, row_sem, *, tt):
    i = pl.program_id(0)
    cp = pltpu.make_async_copy(rows_hbm.at[i], idx_smem, idx_sem)
    cp.start()
    cp.wait()

    def issue(n, c):
        pltpu.make_async_copy(ys_hbm.at[pl.ds(idx_smem[n], 1)], buf.at[pl.ds(n, 1)], row_sem).start()
        return c

    lax.fori_loop(0, tt * TOP_K, issue, 0, unroll=8)
    pltpu.make_async_copy(ys_hbm.at[pl.ds(0, tt * TOP_K)], buf, row_sem).wait()

    wt = wt_ref[...]
    acc = DN_ALPHA * h_ref[...] + sh_ref[...]
    for k in range(TOP_K):
        acc = acc + wt[:, k:k + 1] * buf[k * tt:(k + 1) * tt, :]
    o_ref[...] = _layernorm(acc, g_ref[...], b_ref[...])


def _combine_ln(ys, slot_row, wts, h, shared, g, b, tt=64):
    t, d = h.shape
    tt = min(tt, t)
    steps = t // tt
    blk = pl.BlockSpec((tt, d), lambda i: (i, 0))
    vec = pl.BlockSpec((1, d), lambda i: (0, 0))
    return pl.pallas_call(
        functools.partial(_combine_kernel, tt=tt),
        grid=(steps,),
        in_specs=[pl.BlockSpec(memory_space=pl.ANY), pl.BlockSpec(memory_space=pl.ANY),
                  pl.BlockSpec((tt, TOP_K), lambda i: (i, 0)), blk, blk, vec, vec],
        out_specs=blk,
        out_shape=jax.ShapeDtypeStruct((t, d), F32),
        scratch_shapes=[pltpu.SMEM((tt * TOP_K,), I32), pltpu.VMEM((tt * TOP_K, d), F32),
                        pltpu.SemaphoreType.DMA(()), pltpu.SemaphoreType.DMA(())],
        compiler_params=_cp(("arbitrary",)),
        name="moe_combine_ln",
    )(slot_row.reshape(steps, tt, TOP_K).transpose(0, 2, 1).reshape(steps, tt * TOP_K),
      ys, wts, h, shared, g.reshape(1, d), b.reshape(1, d))


def _moe_ln(h, router_w, router_b, w_gu, w_down, ws_gu, ws_down, g, b, tm=512):
    eidx, wts = _router(h, router_w, router_b)
    row_tok, slot_row, tile_e, n_used, _ = _moe_plan(eidx, tm)
    xs = _gather_rows(h, row_tok, n_used, tm)
    ys = _expert_ffn(xs, w_gu.astype(BF16), w_down.astype(BF16), tile_e, n_used, tm)
    shared = _shared_ffn(h, ws_gu.astype(BF16), ws_down.astype(BF16))
    return _combine_ln(ys, slot_row, wts, h, shared, g, b)


def _layer_diff_moba(h, bsz, seq, w_in, lam_q1, lam_k1, lam_q2, lam_k2, subln_g, w_out, ln_g, ln_b, layer_idx):
    w = DIFF_HEADS * HEAD_DIM
    p = _matmul(h, w_in.astype(BF16), 1024, 768)
    dq, dk, dv, mq, mk, mv, kmean = _split0(p, seq)
    lam_init = 0.8 - 0.6 * math.exp(-0.3 * layer_idx)
    lam = (jnp.exp(jnp.sum(lam_q1 * lam_k1)) - jnp.exp(jnp.sum(lam_q2 * lam_k2)) + lam_init).reshape(1)
    sh = lambda a: a.reshape(bsz, seq, w)
    o_a = _diff_attention(lam, sh(dq), sh(dk), sh(dv), subln_g, lam_init)
    o_b = _moba_attention(sh(mq), sh(mk), sh(mv), kmean.reshape(bsz, seq // MOBA_BLOCK, w))
    w_out = w_out.astype(BF16)
    return _proj_ln([o_a.reshape(-1, w), o_b.reshape(-1, w)], [w_out[:w], w_out[w:]], h, ln_g, ln_b)


def _layer_nsa(h, bsz, seq, w_in, pe_k, w1_k, w2_k, pe_v, w1_v, w2_v, w_out, ln_g, ln_b):
    wq = NSA_HEADS * HEAD_DIM
    wkv = NSA_GROUPS * HEAD_DIM
    n_in = w_in.shape[1]
    n_pad = -(-(n_in + LANES) // 768) * 768
    p = _matmul(h, jnp.pad(w_in, ((0, 0), (0, n_pad - n_in))).astype(BF16), 1024, 768)
    q, qr, ks, vs, kw, vw, gt = _split1(p, seq)

    def chunks(col):
        a = p[:, col:col + wkv].reshape(bsz, seq, NSA_GROUPS, HEAD_DIM).transpose(0, 2, 1, 3)
        return a.reshape(bsz * NSA_GROUPS, seq // NSA_CMP_STRIDE, NSA_CMP_STRIDE * HEAD_DIM)

    k_cmp = _compress(chunks(wq), pe_k, w1_k, w2_k)
    v_cmp = _compress(chunks(wq + wkv), pe_v, w1_v, w2_v)
    sq = lambda a: a.reshape(bsz, seq, wq)
    skv = lambda a: a.reshape(bsz, seq, wkv)
    o_cmp, sel = _cmp_select(sq(q), k_cmp, v_cmp)
    o_sel = _sel_attention(sq(qr), skv(ks), skv(vs), sel)
    gates = gt[:, :NSA_HEADS * 3].reshape(bsz, seq, NSA_GROUPS, NSA_REP * 3)
    gates = jnp.pad(gates, ((0, 0), (0, 0), (0, 0), (0, LANES - NSA_REP * 3))).reshape(bsz, seq, NSA_GROUPS * LANES)
    o = _win_merge(sq(qr), skv(kw), skv(vw), o_cmp, o_sel, gates)
    return _proj_ln([o.reshape(-1, wq)], [w_out.astype(BF16)], h, ln_g, ln_b)


def kernel(x, a_w_in, a_lam_q1, a_lam_k1, a_lam_q2, a_lam_k2, a_subln_g, a_w_out, c_w_in, c_pe_k, c_w1_k, c_w2_k, c_pe_v, c_w1_v, c_w2_v, c_w_out, ln_mix_g, ln_mix_b, ln_ffn_g, ln_ffn_b, router_w, router_b, w_gu, w_down, ws_gu, ws_down):
    bsz, seq, d = x.shape
    h = x.reshape(bsz * seq, d)
    depth = ln_mix_g.shape[0]
    for i in range(depth):
        j = i // 2
        if i % 2 == 0:
            h = _layer_diff_moba(h, bsz, seq, a_w_in[j], a_lam_q1[j], a_lam_k1[j], a_lam_q2[j], a_lam_k2[j],
                                 a_subln_g[j], a_w_out[j], ln_mix_g[i], ln_mix_b[i], i)
        else:
            h = _layer_nsa(h, bsz, seq, c_w_in[j], c_pe_k[j], c_w1_k[j], c_w2_k[j], c_pe_v[j], c_w1_v[j],
                           c_w2_v[j], c_w_out[j], ln_mix_g[i], ln_mix_b[i])
        h = _moe_ln(h, router_w[i], router_b[i], w_gu[i], w_down[i], ws_gu[i], ws_down[i],
                    ln_ffn_g[i], ln_ffn_b[i])
    return h.reshape(bsz, seq, d)
```

```python
import functools
import math

import jax
import jax.numpy as jnp
import numpy as np
from jax import lax
from jax.experimental import pallas as pl
from jax.experimental.pallas import tpu as pltpu

F32 = jnp.float32
BF16 = jnp.bfloat16
I32 = jnp.int32

HEAD_DIM = 128
ROPE_THETA = 10000.0
NEG = -1e30
LN_EPS = 1e-5
DEPTH = 2
DN_ALPHA = (2 * DEPTH) ** 0.25

DIFF_HEADS = 8
MOBA_HEADS = 8
MOBA_BLOCK = 256
MOBA_TOPK = 3
NSA_HEADS = 16
NSA_GROUPS = 2
NSA_REP = NSA_HEADS // NSA_GROUPS
NSA_CMP_LEN = 32
NSA_CMP_STRIDE = 16
NSA_SEL_BLOCK = 64
NSA_SEL_N = 16
NSA_WINDOW = 512
NSA_FORCED = 1e4
N_EXPERTS = 64
EXPERT_DIM = 512
TOP_K = 8
N_GROUPS = 8
GROUP_SIZE = N_EXPERTS // N_GROUPS
TOPK_GROUPS = 4
ROUTED_SCALE = 2.5

LANES = 128
ISSUE_GROUP = 16
VMEM_LIMIT = 56 * 1024 * 1024

_NT = (((1,), (1,)), ((), ()))


def _cp(sem, **kw):
    return pltpu.CompilerParams(dimension_semantics=sem, vmem_limit_bytes=VMEM_LIMIT, **kw)


def _layernorm(x, g, b):
    mu = jnp.mean(x, axis=-1, keepdims=True)
    xc = x - mu
    var = jnp.mean(xc * xc, axis=-1, keepdims=True)
    return xc * lax.rsqrt(var + LN_EPS) * g + b


def _mm_kernel(x_ref, w_ref, o_ref):
    o_ref[...] = jnp.dot(x_ref[...].astype(BF16), w_ref[...],
                         preferred_element_type=F32).astype(o_ref.dtype)


def _matmul(x, w, tm, tn, out_dtype=F32):
    m, k = x.shape
    n = w.shape[1]
    tm, tn = min(tm, m), min(tn, n)
    return pl.pallas_call(
        _mm_kernel,
        grid=(m // tm, n // tn),
        in_specs=[pl.BlockSpec((tm, k), lambda i, j: (i, 0)),
                  pl.BlockSpec((k, tn), lambda i, j: (0, j))],
        out_specs=pl.BlockSpec((tm, tn), lambda i, j: (i, j)),
        out_shape=jax.ShapeDtypeStruct((m, n), out_dtype),
        compiler_params=_cp(("parallel", "arbitrary")),
        name="proj_in",
    )(x, w)


def _proj_ln_kernel(*refs, n_in, alpha):
    a_refs, w_refs = refs[:n_in], refs[n_in:2 * n_in]
    h_ref, g_ref, b_ref, o_ref = refs[2 * n_in:]
    acc = alpha * h_ref[...]
    for a_ref, w_ref in zip(a_refs, w_refs):
        acc = acc + jnp.dot(a_ref[...], w_ref[...], preferred_element_type=F32)
    o_ref[...] = _layernorm(acc, g_ref[...], b_ref[...])


def _proj_ln(acts, ws, h, g, b, tm=256):
    m, d = h.shape
    tm = min(tm, m)
    n_in = len(acts)
    in_specs = ([pl.BlockSpec((tm, a.shape[1]), lambda i: (i, 0)) for a in acts]
                + [pl.BlockSpec(w.shape, lambda i: (0, 0)) for w in ws]
                + [pl.BlockSpec((tm, d), lambda i: (i, 0)),
                   pl.BlockSpec((1, d), lambda i: (0, 0)),
                   pl.BlockSpec((1, d), lambda i: (0, 0))])
    return pl.pallas_call(
        functools.partial(_proj_ln_kernel, n_in=n_in, alpha=DN_ALPHA),
        grid=(m // tm,),
        in_specs=in_specs,
        out_specs=pl.BlockSpec((tm, d), lambda i: (i, 0)),
        out_shape=jax.ShapeDtypeStruct((m, d), F32),
        compiler_params=_cp(("parallel",)),
        name="proj_out_ln",
    )(*acts, *ws, h, g.reshape(1, d), b.reshape(1, d))


def _rope_tables(seq, d):
    half = d // 2
    inv = ROPE_THETA ** (-jnp.arange(half, dtype=F32) * 2.0 / d)
    ang = jnp.arange(seq, dtype=F32)[:, None] * inv[None, :]
    cos, sin = jnp.cos(ang), jnp.sin(ang)
    return jnp.concatenate([cos, cos], axis=1), jnp.concatenate([-sin, sin], axis=1)


def _rope128(x, cos, sin_signed):
    return x * cos + pltpu.roll(x, 64, 1) * sin_signed


def _rope64x2(x, cos, sin_lo, sin_hi):
    return x * cos + pltpu.roll(x, 96, 1) * sin_lo + pltpu.roll(x, 32, 1) * sin_hi


def _split0_kernel(p_ref, c64_ref, slo_ref, shi_ref, c128_ref, s128_ref,
                   dq_ref, dk_ref, dv_ref, mq_ref, mk_ref, mv_ref, km_ref, *, rows):
    c64, slo, shi = c64_ref[...], slo_ref[...], shi_ref[...]
    c128, s128 = c128_ref[...], s128_ref[...]
    w = DIFF_HEADS * HEAD_DIM
    dscale = (HEAD_DIM // 2) ** -0.5
    mscale = HEAD_DIM ** -0.5
    for h in range(DIFF_HEADS):
        lo, hi = h * HEAD_DIM, (h + 1) * HEAD_DIM
        dq_ref[:, lo:hi] = (_rope64x2(p_ref[:, lo:hi], c64, slo, shi) * dscale).astype(BF16)
        dk_ref[:, lo:hi] = _rope64x2(p_ref[:, w + lo:w + hi], c64, slo, shi).astype(BF16)
        dv_ref[:, lo:hi] = p_ref[:, 2 * w + lo:2 * w + hi].astype(BF16)
        mq_ref[:, lo:hi] = (_rope128(p_ref[:, 3 * w + lo:3 * w + hi], c128, s128) * mscale).astype(BF16)
        kr = _rope128(p_ref[:, 4 * w + lo:4 * w + hi], c128, s128)
        mk_ref[:, lo:hi] = kr.astype(BF16)
        km_ref[0, :, lo:hi] = jnp.sum(kr, axis=0, keepdims=True) * (1.0 / rows)
        mv_ref[:, lo:hi] = p_ref[:, 5 * w + lo:5 * w + hi].astype(BF16)


def _split0(p, seq):
    t = p.shape[0]
    rows = MOBA_BLOCK
    w = DIFF_HEADS * HEAD_DIM
    nb = seq // rows
    c64h, s64h = _rope_tables(seq, HEAD_DIM // 2)
    c64 = jnp.concatenate([c64h, c64h], axis=1)
    lane = jnp.arange(HEAD_DIM) % (HEAD_DIM // 2)
    s64 = jnp.concatenate([s64h, s64h], axis=1)
    slo = jnp.where(lane < HEAD_DIM // 4, s64, 0.0)
    shi = jnp.where(lane >= HEAD_DIM // 4, s64, 0.0)
    c128, s128 = _rope_tables(seq, HEAD_DIM)
    tab = pl.BlockSpec((rows, HEAD_DIM), lambda i: (i % nb, 0))
    out = pl.BlockSpec((rows, w), lambda i: (i, 0))
    bf = jax.ShapeDtypeStruct((t, w), BF16)
    return pl.pallas_call(
        functools.partial(_split0_kernel, rows=rows),
        grid=(t // rows,),
        in_specs=[pl.BlockSpec((rows, 6 * w), lambda i: (i, 0)), tab, tab, tab, tab, tab],
        out_specs=[out] * 6 + [pl.BlockSpec((1, 1, w), lambda i: (i, 0, 0))],
        out_shape=[bf] * 6 + [jax.ShapeDtypeStruct((t // rows, 1, w), F32)],
        compiler_params=_cp(("parallel",)),
        name="split_rope_l0",
    )(p, c64, slo, shi, c128, s128)


def _flash_update(carry, s, v):
    m, l, acc = carry
    m_new = jnp.maximum(m, jnp.max(s, axis=1, keepdims=True))
    p = jnp.exp(s - m_new)
    a = jnp.exp(m - m_new)
    l = a * l + jnp.sum(p, axis=1, keepdims=True)
    acc = a * acc + jnp.dot(p.astype(BF16), v, preferred_element_type=F32)
    return m_new, l, acc


def _flash_init(rows, d):
    return (jnp.full((rows, 1), NEG, F32), jnp.zeros((rows, 1), F32), jnp.zeros((rows, d), F32))


def _diff_kernel(lam_ref, q_ref, k_ref, v_ref, g_ref, o_ref, *, tq, out_scale):
    i = pl.program_id(2)
    q = q_ref[0]
    lane = lax.broadcasted_iota(I32, (tq, HEAD_DIM), 1)
    zero = jnp.zeros_like(q)
    qs = jnp.concatenate([jnp.where(lane < HEAD_DIM // 2, q, zero),
                          jnp.where(lane >= HEAD_DIM // 2, q, zero)], axis=0)

    def tile(j):
        st = pl.multiple_of(j * tq, tq)
        k = k_ref[0, pl.ds(st, tq), :]
        v = v_ref[0, pl.ds(st, tq), :]
        return lax.dot_general(qs, k, _NT, preferred_element_type=F32), v

    def body(j, carry):
        s, v = tile(j)
        return _flash_update(carry, s, v)

    carry = lax.fori_loop(0, i, body, _flash_init(2 * tq, HEAD_DIM))
    s, v = tile(i)
    r = lax.broadcasted_iota(I32, (2 * tq, tq), 0)
    c = lax.broadcasted_iota(I32, (2 * tq, tq), 1)
    s = jnp.where(c <= jnp.where(r >= tq, r - tq, r), s, NEG)
    _, l, acc = _flash_update(carry, s, v)
    o = acc / l
    od = o[:tq] - lam_ref[0] * o[tq:]
    od = od * lax.rsqrt(jnp.mean(od * od, axis=1, keepdims=True) + LN_EPS) * g_ref[...] * out_scale
    o_ref[0] = od.astype(o_ref.dtype)


def _diff_attention(lam, q, k, v, subln_g, lam_init, tq=256):
    b, s, _ = q.shape
    tq = min(tq, s)
    kv = pl.BlockSpec((1, s, HEAD_DIM), lambda bi, h, i, *_: (bi, 0, h))
    qo = pl.BlockSpec((1, tq, HEAD_DIM), lambda bi, h, i, *_: (bi, i, h))
    return pl.pallas_call(
        functools.partial(_diff_kernel, tq=tq, out_scale=1.0 - lam_init),
        grid_spec=pltpu.PrefetchScalarGridSpec(
            num_scalar_prefetch=1,
            grid=(b, DIFF_HEADS, s // tq),
            in_specs=[qo, kv, kv, pl.BlockSpec((1, HEAD_DIM), lambda bi, h, i, *_: (0, 0))],
            out_specs=qo),
        out_shape=jax.ShapeDtypeStruct(q.shape, BF16),
        compiler_params=_cp(("parallel", "parallel", "arbitrary")),
        name="diff_attention",
    )(lam, q, k, v, subln_g.reshape(1, HEAD_DIM))


def _moba_kernel(q_ref, k_ref, v_ref, km_ref, o_ref, *, blk, nb, n_pick, hp):
    i = pl.program_id(2)
    bid = lax.broadcasted_iota(I32, (blk, nb), 1)
    past = bid < i
    r = lax.broadcasted_iota(I32, (blk, blk), 0)
    c = lax.broadcasted_iota(I32, (blk, blk), 1)
    heads = [slice(h * HEAD_DIM, (h + 1) * HEAD_DIM) for h in range(hp)]

    def tile(hs, q, j):
        st = pl.multiple_of(j * blk, blk)
        k = k_ref[0, pl.ds(st, blk), hs]
        v = v_ref[0, pl.ds(st, blk), hs]
        return lax.dot_general(q, k, _NT, preferred_element_type=F32), v

    qs, sels, carries = [], [], []
    for hs in heads:
        q = q_ref[0, :, hs]
        gate = lax.dot_general(q.astype(F32), km_ref[0, :, hs], _NT, precision=lax.Precision.HIGHEST,
                               preferred_element_type=F32)
        g = jnp.where(past, gate, NEG)
        sel = jnp.zeros((blk, nb), F32)
        for _ in range(n_pick):
            mx = jnp.max(g, axis=1, keepdims=True)
            idx = jnp.min(jnp.where(g == mx, bid, nb), axis=1, keepdims=True)
            pick = bid == idx
            sel = jnp.where(pick, jnp.where(past, 1.0, 0.0), sel)
            g = jnp.where(pick, -jnp.inf, g)
        s, v = tile(hs, q, i)
        carries.append(_flash_update(_flash_init(blk, HEAD_DIM), jnp.where(c <= r, s, NEG), v))
        qs.append(q)
        sels.append(sel)

    def body(j, carries):
        out = []
        for hs, q, sel, carry in zip(heads, qs, sels, carries):
            s, v = tile(hs, q, j)
            on = jnp.sum(jnp.where(bid == j, sel, 0.0), axis=1, keepdims=True) > 0.5
            out.append(_flash_update(carry, jnp.where(on, s, NEG), v))
        return tuple(out)

    carries = lax.fori_loop(0, i, body, tuple(carries))
    for hs, (_, l, acc) in zip(heads, carries):
        o_ref[0, :, hs] = (acc / l).astype(o_ref.dtype)


def _moba_attention(q, k, v, kmean, hp=2):
    b, s, _ = q.shape
    blk = min(MOBA_BLOCK, s)
    nb = s // blk
    n_pick = max(1, min(MOBA_TOPK, nb - 1))
    kv = pl.BlockSpec((1, s, hp * HEAD_DIM), lambda bi, h, i: (bi, 0, h))
    qo = pl.BlockSpec((1, blk, hp * HEAD_DIM), lambda bi, h, i: (bi, i, h))
    return pl.pallas_call(
        functools.partial(_moba_kernel, blk=blk, nb=nb, n_pick=n_pick, hp=hp),
        grid=(b, MOBA_HEADS // hp, nb),
        in_specs=[qo, kv, kv, pl.BlockSpec((1, nb, hp * HEAD_DIM), lambda bi, h, i: (bi, 0, h))],
        out_specs=qo,
        out_shape=jax.ShapeDtypeStruct(q.shape, BF16),
        compiler_params=_cp(("parallel", "parallel", "arbitrary")),
        name="moba_attention",
    )(q, k, v, kmean)


def _split1_kernel(p_ref, c_ref, s_ref, q_ref, qr_ref, ks_ref, vs_ref, kw_ref, vw_ref, gt_ref):
    cos, sin = c_ref[...], s_ref[...]
    scale = HEAD_DIM ** -0.5
    wq = NSA_HEADS * HEAD_DIM
    wkv = NSA_GROUPS * HEAD_DIM
    for h in range(NSA_HEADS):
        lo, hi = h * HEAD_DIM, (h + 1) * HEAD_DIM
        x = p_ref[:, lo:hi]
        q_ref[:, lo:hi] = (x * scale).astype(BF16)
        qr_ref[:, lo:hi] = (_rope128(x, cos, sin) * scale).astype(BF16)
    base = wq + 2 * wkv
    for g in range(NSA_GROUPS):
        lo, hi = g * HEAD_DIM, (g + 1) * HEAD_DIM
        ks_ref[:, lo:hi] = _rope128(p_ref[:, base + lo:base + hi], cos, sin).astype(BF16)
        vs_ref[:, lo:hi] = p_ref[:, base + wkv + lo:base + wkv + hi].astype(BF16)
        kw_ref[:, lo:hi] = _rope128(p_ref[:, base + 2 * wkv + lo:base + 2 * wkv + hi], cos, sin).astype(BF16)
        vw_ref[:, lo:hi] = p_ref[:, base + 3 * wkv + lo:base + 3 * wkv + hi].astype(BF16)
    gt_ref[...] = jax.nn.sigmoid(p_ref[:, base + 4 * wkv:base + 4 * wkv + LANES])


def _split1(p, seq, rows=256):
    t, wp = p.shape
    rows = min(rows, seq)
    nb = seq // rows
    wq = NSA_HEADS * HEAD_DIM
    wkv = NSA_GROUPS * HEAD_DIM
    cos, sin = _rope_tables(seq, HEAD_DIM)
    tab = pl.BlockSpec((rows, HEAD_DIM), lambda i: (i % nb, 0))
    oq = pl.BlockSpec((rows, wq), lambda i: (i, 0))
    okv = pl.BlockSpec((rows, wkv), lambda i: (i, 0))
    sq = jax.ShapeDtypeStruct((t, wq), BF16)
    skv = jax.ShapeDtypeStruct((t, wkv), BF16)
    return pl.pallas_call(
        _split1_kernel,
        grid=(t // rows,),
        in_specs=[pl.BlockSpec((rows, wp), lambda i: (i, 0)), tab, tab],
        out_specs=[oq, oq, okv, okv, okv, okv, pl.BlockSpec((rows, LANES), lambda i: (i, 0))],
        out_shape=[sq, sq, skv, skv, skv, skv, jax.ShapeDtypeStruct((t, LANES), F32)],
        compiler_params=_cp(("parallel",)),
        name="split_rope_l1",
    )(p, cos, sin)


def _gelu_tanh(x):
    return 0.5 * x * (1.0 + jnp.tanh(math.sqrt(2.0 / math.pi) * (x + 0.044715 * (x * x * x))))


def _compress_kernel(c_ref, pelo_ref, pehi_ref, w1lo_ref, w1hi_ref, w2_ref, o_ref, *, nc):
    c = c_ref[0]
    a = jnp.dot((c + pelo_ref[...]).astype(BF16), w1lo_ref[...], preferred_element_type=F32)
    bm = jnp.dot((c + pehi_ref[...]).astype(BF16), w1hi_ref[...], preferred_element_type=F32)
    hid = _gelu_tanh(a + pltpu.roll(bm, nc - 1, 0))
    o_ref[0] = jnp.dot(hid.astype(BF16), w2_ref[...], preferred_element_type=F32).astype(o_ref.dtype)


def _compress(chunks, pe, w1, w2):
    bg, nc, cw = chunks.shape
    half = NSA_CMP_STRIDE
    pelo = pe[:half].reshape(1, cw)
    pehi = pe[half:].reshape(1, cw)
    w1lo = w1[:cw].astype(BF16)
    w1hi = w1[cw:].astype(BF16)
    full = lambda a: pl.BlockSpec(a.shape, lambda i: (0,) * a.ndim)
    return pl.pallas_call(
        functools.partial(_compress_kernel, nc=nc),
        grid=(bg,),
        in_specs=[pl.BlockSpec((1, nc, cw), lambda i: (i, 0, 0)), full(pelo), full(pehi),
                  full(w1lo), full(w1hi), pl.BlockSpec(w2.shape, lambda i: (0, 0))],
        out_specs=pl.BlockSpec((1, nc, HEAD_DIM), lambda i: (i, 0, 0)),
        out_shape=jax.ShapeDtypeStruct((bg, nc, HEAD_DIM), BF16),
        compiler_params=_cp(("parallel",)),
        name="nsa_compress",
    )(chunks, pelo, pehi, w1lo, w1hi, w2.astype(BF16))


def _stack_heads(q):
    return jnp.concatenate([q[:, r * HEAD_DIM:(r + 1) * HEAD_DIM] for r in range(NSA_REP)], axis=0)


def _unstack_store(o_ref, o, tq):
    for r in range(NSA_REP):
        o_ref[0, :, r * HEAD_DIM:(r + 1) * HEAD_DIM] = o[r * tq:(r + 1) * tq].astype(o_ref.dtype)


def _cmp_select_kernel(q_ref, kc_ref, vc_ref, o_ref, sel_ref, *, tq, nc, nblk, n_pick):
    i = pl.program_id(2)
    qs = _stack_heads(q_ref[0])
    s = lax.dot_general(qs, kc_ref[0], _NT, preferred_element_type=F32)
    rows = NSA_REP * tq
    row = lax.broadcasted_iota(I32, (rows, nc), 0)
    n = lax.broadcasted_iota(I32, (rows, nc), 1)
    pos = i * tq + (row & (tq - 1))
    valid = n * NSA_CMP_STRIDE + (NSA_CMP_LEN - 1) <= pos
    sm = jnp.where(valid, s, NEG)
    e = jnp.exp(sm - jnp.max(sm, axis=1, keepdims=True))
    p = jnp.where(valid, e / jnp.sum(e, axis=1, keepdims=True), 0.0)
    o = jnp.dot(p.astype(BF16), vc_ref[0], preferred_element_type=F32)
    _unstack_store(o_ref, o, tq)

    psum = p[0:tq]
    for r in range(1, NSA_REP):
        psum = psum + p[r * tq:(r + 1) * tq]
    cn = lax.broadcasted_iota(I32, (nc, nblk), 0) * NSA_CMP_STRIDE
    bj = lax.broadcasted_iota(I32, (nc, nblk), 1) * NSA_SEL_BLOCK
    ov = jnp.maximum(jnp.minimum(cn + NSA_CMP_LEN, bj + NSA_SEL_BLOCK) - jnp.maximum(cn, bj), 0)
    ov = ov.astype(F32) * (1.0 / NSA_CMP_LEN)
    imp = jnp.dot(psum, ov, precision=lax.Precision.HIGHEST, preferred_element_type=F32)

    cur = (i * tq + lax.broadcasted_iota(I32, (tq, nblk), 0)) // NSA_SEL_BLOCK
    blk = lax.broadcasted_iota(I32, (tq, nblk), 1)
    allowed = blk <= cur
    forced = (blk == 0) | (blk == cur) | (blk == cur - 1)
    val = jnp.where(allowed, jnp.where(forced, NSA_FORCED, imp), NEG)
    sel = jnp.zeros((tq, nblk), F32)
    for _ in range(n_pick):
        mx = jnp.max(val, axis=1, keepdims=True)
        idx = jnp.min(jnp.where(val == mx, blk, nblk), axis=1, keepdims=True)
        pick = blk == idx
        sel = jnp.where(pick, 1.0, sel)
        val = jnp.where(pick, -jnp.inf, val)
    sel_ref[0, 0] = jnp.where(allowed, sel, 0.0)


def _cmp_select(q, kcmp, vcmp, tq=128):
    b, s, _ = q.shape
    tq = min(tq, s)
    nc = kcmp.shape[1]
    nblk = s // NSA_SEL_BLOCK
    n_pick = min(NSA_SEL_N, nblk)
    wg = NSA_REP * HEAD_DIM
    qo = pl.BlockSpec((1, tq, wg), lambda bi, g, i: (bi, i, g))
    kv = pl.BlockSpec((1, nc, HEAD_DIM), lambda bi, g, i: (bi * NSA_GROUPS + g, 0, 0))
    return pl.pallas_call(
        functools.partial(_cmp_select_kernel, tq=tq, nc=nc, nblk=nblk, n_pick=n_pick),
        grid=(b, NSA_GROUPS, s // tq),
        in_specs=[qo, kv, kv],
        out_specs=[qo, pl.BlockSpec((1, 1, tq, nblk), lambda bi, g, i: (bi, g, i, 0))],
        out_shape=[jax.ShapeDtypeStruct(q.shape, F32),
                   jax.ShapeDtypeStruct((b, NSA_GROUPS, s, nblk), F32)],
        compiler_params=_cp(("parallel", "parallel", "arbitrary")),
        name="nsa_cmp_select",
    )(q, kcmp, vcmp)


def _mask_heads(s, keep, tq):
    tk = s.shape[1]
    s3 = jnp.where(keep[None], s.reshape(NSA_REP, tq, tk), NEG)
    return s3.reshape(NSA_REP * tq, tk)


def _sel_kernel(q_ref, k_ref, v_ref, sel_ref, o_ref, *, tq, tk, nblk):
    i = pl.program_id(2)
    qs = _stack_heads(q_ref[0])
    sel = sel_ref[0, 0].astype(BF16)
    per = tk // NSA_SEL_BLOCK
    qpos = i * tq + lax.broadcasted_iota(I32, (tq, tk), 0)
    kcol = lax.broadcasted_iota(I32, (tq, tk), 1)
    eb = lax.broadcasted_iota(I32, (nblk, tk), 0)
    ec = lax.broadcasted_iota(I32, (nblk, tk), 1) // NSA_SEL_BLOCK

    def body(j, carry):
        st = pl.multiple_of(j * tk, tk)
        k = k_ref[0, pl.ds(st, tk), :]
        v = v_ref[0, pl.ds(st, tk), :]
        s = lax.dot_general(qs, k, _NT, preferred_element_type=F32)
        expand = jnp.where(eb == j * per + ec, 1.0, 0.0).astype(BF16)
        on = jnp.dot(sel, expand, preferred_element_type=F32)
        keep = jnp.where(j * tk + kcol <= qpos, on, 0.0) > 0.5
        return _flash_update(carry, _mask_heads(s, keep, tq), v)

    n_tiles = (i * tq + tq - 1) // tk + 1
    _, l, acc = lax.fori_loop(0, n_tiles, body, _flash_init(NSA_REP * tq, HEAD_DIM))
    _unstack_store(o_ref, acc / l, tq)


def _sel_attention(q, k, v, sel, tq=128, tk=512):
    b, s, _ = q.shape
    tq, tk = min(tq, s), min(tk, s)
    nblk = s // NSA_SEL_BLOCK
    wg = NSA_REP * HEAD_DIM
    qo = pl.BlockSpec((1, tq, wg), lambda bi, g, i: (bi, i, g))
    kv = pl.BlockSpec((1, s, HEAD_DIM), lambda bi, g, i: (bi, 0, g))
    return pl.pallas_call(
        functools.partial(_sel_kernel, tq=tq, tk=tk, nblk=nblk),
        grid=(b, NSA_GROUPS, s // tq),
        in_specs=[qo, kv, kv, pl.BlockSpec((1, 1, tq, nblk), lambda bi, g, i: (bi, g, i, 0))],
        out_specs=qo,
        out_shape=jax.ShapeDtypeStruct(q.shape, F32),
        compiler_params=_cp(("parallel", "parallel", "arbitrary")),
        name="nsa_sel_attention",
    )(q, k, v, sel)


def _win_kernel(q_ref, k_ref, v_ref, oc_ref, os_ref, gt_ref, o_ref, *, tq, window):
    i = pl.program_id(2)
    qs = _stack_heads(q_ref[0])
    qpos = i * tq + lax.broadcasted_iota(I32, (tq, tq), 0)
    kcol = lax.broadcasted_iota(I32, (tq, tq), 1)

    def body(j, carry):
        st = pl.multiple_of(j * tq, tq)
        k = k_ref[0, pl.ds(st, tq), :]
        v = v_ref[0, pl.ds(st, tq), :]
        s = lax.dot_general(qs, k, _NT, preferred_element_type=F32)
        kpos = j * tq + kcol
        keep = (kpos <= qpos) & (kpos > qpos - window)
        return _flash_update(carry, _mask_heads(s, keep, tq), v)

    def rev_body(t, carry):
        return body(i - t, carry)

    n_tiles = jnp.minimum(i, window // tq) + 1
    _, l, acc = lax.fori_loop(0, n_tiles, rev_body, _flash_init(NSA_REP * tq, HEAD_DIM))
    ow = acc / l
    gt = gt_ref[0]
    for r in range(NSA_REP):
        lo, hi = r * HEAD_DIM, (r + 1) * HEAD_DIM
        mix = (gt[:, 3 * r:3 * r + 1] * oc_ref[0, :, lo:hi]
               + gt[:, 3 * r + 1:3 * r + 2] * os_ref[0, :, lo:hi]
               + gt[:, 3 * r + 2:3 * r + 3] * ow[r * tq:(r + 1) * tq])
        o_ref[0, :, lo:hi] = mix.astype(o_ref.dtype)


def _win_merge(q, k, v, o_cmp, o_sel, gates, tq=128):
    b, s, _ = q.shape
    tq = min(tq, s)
    wg = NSA_REP * HEAD_DIM
    qo = pl.BlockSpec((1, tq, wg), lambda bi, g, i: (bi, i, g))
    kv = pl.BlockSpec((1, s, HEAD_DIM), lambda bi, g, i: (bi, 0, g))
    return pl.pallas_call(
        functools.partial(_win_kernel, tq=tq, window=NSA_WINDOW),
        grid=(b, NSA_GROUPS, s // tq),
        in_specs=[qo, kv, kv, qo, qo, pl.BlockSpec((1, tq, LANES), lambda bi, g, i: (bi, i, g))],
        out_specs=qo,
        out_shape=jax.ShapeDtypeStruct(q.shape, BF16),
        compiler_params=_cp(("parallel", "parallel", "arbitrary")),
        name="nsa_window_merge",
    )(q, k, v, o_cmp, o_sel, gates)


def _router_kernel(x_ref, w_ref, b_ref, idx_ref, wt_ref):
    tm = x_ref.shape[0]
    logits = jnp.dot(x_ref[...], w_ref[...], precision=lax.Precision.HIGHEST,
                     preferred_element_type=F32)
    score = jax.nn.sigmoid(logits)
    biased = score + b_ref[...]
    lane = lax.broadcasted_iota(I32, (tm, N_EXPERTS), 1)
    grp = lane // GROUP_SIZE

    gscore = jnp.zeros((tm, N_EXPERTS), F32)
    for g in range(N_GROUPS):
        ing = grp == g
        v = jnp.where(ing, biased, -jnp.inf)
        m1 = jnp.max(v, axis=1, keepdims=True)
        i1 = jnp.min(jnp.where(v == m1, lane, N_EXPERTS), axis=1, keepdims=True)
        m2 = jnp.max(jnp.where(lane == i1, -jnp.inf, v), axis=1, keepdims=True)
        gscore = jnp.where(ing, m1 + m2, gscore)

    emask = jnp.zeros((tm, N_EXPERTS), F32)
    for _ in range(TOPK_GROUPS):
        mx = jnp.max(gscore, axis=1, keepdims=True)
        gi = jnp.min(jnp.where(gscore == mx, grp, N_GROUPS), axis=1, keepdims=True)
        pick = grp == gi
        emask = jnp.where(pick, 1.0, emask)
        gscore = jnp.where(pick, -jnp.inf, gscore)

    val = jnp.where(emask > 0.5, biased, NEG)
    kcol = lax.broadcasted_iota(I32, (tm, TOP_K), 1)
    idx_out = jnp.zeros((tm, TOP_K), I32)
    wt_out = jnp.zeros((tm, TOP_K), F32)
    for kk in range(TOP_K):
        mx = jnp.max(val, axis=1, keepdims=True)
        ei = jnp.min(jnp.where(val == mx, lane, N_EXPERTS), axis=1, keepdims=True)
        pick = lane == ei
        wk = jnp.sum(jnp.where(pick, score, 0.0), axis=1, keepdims=True)
        idx_out = jnp.where(kcol == kk, ei, idx_out)
        wt_out = jnp.where(kcol == kk, wk, wt_out)
        val = jnp.where(pick, -jnp.inf, val)
    idx_ref[...] = idx_out
    wt_ref[...] = wt_out / jnp.sum(wt_out, axis=1, keepdims=True) * ROUTED_SCALE


def _router(x, w, bias, tm=256):
    t, d = x.shape
    tm = min(tm, t)
    return pl.pallas_call(
        _router_kernel,
        grid=(t // tm,),
        in_specs=[pl.BlockSpec((tm, d), lambda i: (i, 0)),
                  pl.BlockSpec((d, N_EXPERTS), lambda i: (0, 0)),
                  pl.BlockSpec((1, N_EXPERTS), lambda i: (0, 0))],
        out_specs=[pl.BlockSpec((tm, TOP_K), lambda i: (i, 0)),
                   pl.BlockSpec((tm, TOP_K), lambda i: (i, 0))],
        out_shape=[jax.ShapeDtypeStruct((t, TOP_K), I32), jax.ShapeDtypeStruct((t, TOP_K), F32)],
        compiler_params=_cp(("parallel",)),
        name="moe_router",
    )(x, w, bias.reshape(1, N_EXPERTS))


def _moe_plan(eidx, tm):
    t = eidx.shape[0]
    n_slots = t * TOP_K
    flat_e = eidx.reshape(-1)
    order = jnp.argsort(flat_e, stable=True).astype(I32)
    rank = jnp.argsort(order).astype(I32)
    experts = jnp.arange(N_EXPERTS, dtype=I32)
    counts = jnp.sum((flat_e[:, None] == experts[None, :]).astype(I32), axis=0)
    pcounts = (counts + tm - 1) // tm * tm
    pend = jnp.cumsum(pcounts)
    pstart = pend - pcounts
    cstart = jnp.cumsum(counts) - counts
    n_rows = n_slots + N_EXPERTS * tm
    n_tiles = n_rows // tm
    tile_start = jnp.arange(n_tiles, dtype=I32) * tm
    tile_e = jnp.clip(jnp.sum((pend[None, :] <= tile_start[:, None]).astype(I32), axis=1), 0, N_EXPERTS - 1)
    j = tile_start[:, None] + jnp.arange(tm, dtype=I32)[None, :] - pstart[tile_e][:, None]
    src = jnp.clip(cstart[tile_e][:, None] + j, 0, n_slots - 1)
    row_tok = jnp.where(j < counts[tile_e][:, None], order[src] // TOP_K, 0).astype(I32)
    slot_row = ((pstart - cstart)[flat_e] + rank).astype(I32)
    n_used = (pend[-1] // tm).astype(I32).reshape(1)
    return row_tok, slot_row, tile_e.astype(I32), n_used


def _gather_rows_pipelined(i, n_steps, idx_hbm, src_hbm, idx_smem, buf, idx_sem, row_sem, rows):
    slot = i % 2

    def idx_copy(t, s):
        return pltpu.make_async_copy(idx_hbm.at[t], idx_smem[s], idx_sem.at[s])

    def issue_rows(s):
        def issue(g, c):
            for k in range(ISSUE_GROUP):
                pltpu.make_async_copy(src_hbm.at[pl.ds(idx_smem[s][g * ISSUE_GROUP + k], 1)],
                                      buf.at[s, g * (ISSUE_GROUP // 8) + k // 8, pl.ds(k % 8, 1)],
                                      row_sem.at[s]).start()
            return c
        lax.fori_loop(0, rows // ISSUE_GROUP, issue, 0)

    @pl.when(i == 0)
    def _():
        idx_copy(0, 0).start()
        idx_copy(0, 0).wait()
        issue_rows(0)

        @pl.when(n_steps > 1)
        def _():
            idx_copy(1, 1).start()

    for s in range(2):
        @pl.when(jnp.logical_and(i + 1 < n_steps, slot == 1 - s))
        def _():
            idx_copy(i + 1, s).wait()
            issue_rows(s)

            @pl.when(i + 2 < n_steps)
            def _():
                idx_copy(i + 2, 1 - s).start()

    @pl.when(i < n_steps)
    def _():
        pltpu.make_async_copy(buf.at[slot], buf.at[slot], row_sem.at[slot]).wait()


def _expert_kernel(te_ref, nused_ref, tok_hbm, x_hbm, wgu_ref, wd_ref, o_ref,
                   idx0, idx1, xbuf, idx_sem, row_sem, *, tm):
    i = pl.program_id(0)
    n_used = nused_ref[0]
    d = x_hbm.shape[1]
    _gather_rows_pipelined(i, n_used, tok_hbm, x_hbm, (idx0, idx1), xbuf, idx_sem, row_sem, tm)

    @pl.when(i < n_used)
    def _():
        gu = jnp.dot(xbuf[i % 2].reshape(tm, d).astype(BF16), wgu_ref[0], preferred_element_type=F32)
        act = jax.nn.silu(gu[:, :EXPERT_DIM]) * gu[:, EXPERT_DIM:]
        o_ref[...] = jnp.dot(act.astype(BF16), wd_ref[0], preferred_element_type=F32)

    @pl.when(i >= n_used)
    def _():
        o_ref[...] = jnp.zeros_like(o_ref)


def _expert_ffn(x, row_tok, w_gu, w_down, tile_e, n_used):
    n_tiles, tm = row_tok.shape
    d = x.shape[1]
    last = lambda i, te, nu: jnp.minimum(i, nu[0] - 1)
    return pl.pallas_call(
        functools.partial(_expert_kernel, tm=tm),
        grid_spec=pltpu.PrefetchScalarGridSpec(
            num_scalar_prefetch=2,
            grid=(n_tiles,),
            in_specs=[pl.BlockSpec(memory_space=pl.ANY), pl.BlockSpec(memory_space=pl.ANY),
                      pl.BlockSpec((1, d, 2 * EXPERT_DIM), lambda i, te, nu: (te[last(i, te, nu)], 0, 0)),
                      pl.BlockSpec((1, EXPERT_DIM, d), lambda i, te, nu: (te[last(i, te, nu)], 0, 0))],
            out_specs=pl.BlockSpec((tm, d), lambda i, te, nu: (i, 0)),
            scratch_shapes=[pltpu.SMEM((tm,), I32), pltpu.SMEM((tm,), I32), pltpu.VMEM((2, tm // 8, 8, d), F32),
                            pltpu.SemaphoreType.DMA((2,)), pltpu.SemaphoreType.DMA((2,))]),
        out_shape=jax.ShapeDtypeStruct((n_tiles * tm, d), F32),
        compiler_params=_cp(("arbitrary",)),
        name="moe_expert_ffn",
    )(tile_e, n_used, row_tok, x, w_gu, w_down)


def _shared_kernel(x_ref, wgu_ref, wd_ref, o_ref):
    gu = jnp.dot(x_ref[...].astype(BF16), wgu_ref[...], preferred_element_type=F32)
    act = jax.nn.silu(gu[:, :EXPERT_DIM]) * gu[:, EXPERT_DIM:]
    o_ref[...] = jnp.dot(act.astype(BF16), wd_ref[...], preferred_element_type=F32)


def _shared_ffn(x, w_gu, w_down, tm=512):
    t, d = x.shape
    tm = min(tm, t)
    return pl.pallas_call(
        _shared_kernel,
        grid=(t // tm,),
        in_specs=[pl.BlockSpec((tm, d), lambda i: (i, 0)),
                  pl.BlockSpec(w_gu.shape, lambda i: (0, 0)),
                  pl.BlockSpec(w_down.shape, lambda i: (0, 0))],
        out_specs=pl.BlockSpec((tm, d), lambda i: (i, 0)),
        out_shape=jax.ShapeDtypeStruct((t, d), F32),
        compiler_params=_cp(("parallel",)),
        name="moe_shared_ffn",
    )(x, w_gu, w_down)


def _combine_kernel(rows_hbm, ys_hbm, wt_ref, h_ref, sh_ref, g_ref, b_ref, o_ref,
                    idx0, idx1, buf, idx_sem, row_sem, *, tt, steps):
    i = pl.program_id(0)
    d = h_ref.shape[1]
    _gather_rows_pipelined(i, steps, rows_hbm, ys_hbm, (idx0, idx1), buf, idx_sem, row_sem, tt * TOP_K)
    wt = wt_ref[...]
    acc = DN_ALPHA * h_ref[...] + sh_ref[...]
    for k in range(TOP_K):
        yk = buf[i % 2, k * (tt // 8):(k + 1) * (tt // 8)].reshape(tt, d)
        acc = acc + wt[:, k:k + 1] * yk
    o_ref[...] = _layernorm(acc, g_ref[...], b_ref[...])


def _combine_ln(ys, slot_row, wts, h, shared, g, b, tt=64):
    t, d = h.shape
    tt = min(tt, t)
    steps = t // tt
    blk = pl.BlockSpec((tt, d), lambda i: (i, 0))
    vec = pl.BlockSpec((1, d), lambda i: (0, 0))
    return pl.pallas_call(
        functools.partial(_combine_kernel, tt=tt, steps=steps),
        grid=(steps,),
        in_specs=[pl.BlockSpec(memory_space=pl.ANY), pl.BlockSpec(memory_space=pl.ANY),
                  pl.BlockSpec((tt, TOP_K), lambda i: (i, 0)), blk, blk, vec, vec],
        out_specs=blk,
        out_shape=jax.ShapeDtypeStruct((t, d), F32),
        scratch_shapes=[pltpu.SMEM((tt * TOP_K,), I32), pltpu.SMEM((tt * TOP_K,), I32),
                        pltpu.VMEM((2, tt * TOP_K // 8, 8, d), F32),
                        pltpu.SemaphoreType.DMA((2,)), pltpu.SemaphoreType.DMA((2,))],
        compiler_params=_cp(("arbitrary",)),
        name="moe_combine_ln",
    )(slot_row.reshape(steps, tt, TOP_K).transpose(0, 2, 1).reshape(steps, tt * TOP_K),
      ys, wts, h, shared, g.reshape(1, d), b.reshape(1, d))


def _moe_ln(h, router_w, router_b, w_gu, w_down, ws_gu, ws_down, g, b, tm=512):
    eidx, wts = _router(h, router_w, router_b)
    row_tok, slot_row, tile_e, n_used = _moe_plan(eidx, tm)
    ys = _expert_ffn(h, row_tok, w_gu.astype(BF16), w_down.astype(BF16), tile_e, n_used)
    shared = _shared_ffn(h, ws_gu.astype(BF16), ws_down.astype(BF16))
    return _combine_ln(ys, slot_row, wts, h, shared, g, b)


def _layer_diff_moba(h, bsz, seq, w_in, lam_q1, lam_k1, lam_q2, lam_k2, subln_g, w_out, ln_g, ln_b, layer_idx):
    w = DIFF_HEADS * HEAD_DIM
    p = _matmul(h, w_in.astype(BF16), 1024, 768)
    dq, dk, dv, mq, mk, mv, kmean = _split0(p, seq)
    lam_init = 0.8 - 0.6 * math.exp(-0.3 * layer_idx)
    lam = (jnp.exp(jnp.sum(lam_q1 * lam_k1)) - jnp.exp(jnp.sum(lam_q2 * lam_k2)) + lam_init).reshape(1)
    sh = lambda a: a.reshape(bsz, seq, w)
    o_a = _diff_attention(lam, sh(dq), sh(dk), sh(dv), subln_g, lam_init)
    o_b = _moba_attention(sh(mq), sh(mk), sh(mv), kmean.reshape(bsz, seq // MOBA_BLOCK, w))
    w_out = w_out.astype(BF16)
    return _proj_ln([o_a.reshape(-1, w), o_b.reshape(-1, w)], [w_out[:w], w_out[w:]], h, ln_g, ln_b)


def _layer_nsa(h, bsz, seq, w_in, pe_k, w1_k, w2_k, pe_v, w1_v, w2_v, w_out, ln_g, ln_b):
    wq = NSA_HEADS * HEAD_DIM
    wkv = NSA_GROUPS * HEAD_DIM
    n_in = w_in.shape[1]
    n_pad = -(-(n_in + LANES) // 768) * 768
    p = _matmul(h, jnp.pad(w_in, ((0, 0), (0, n_pad - n_in))).astype(BF16), 1024, 768)
    q, qr, ks, vs, kw, vw, gt = _split1(p, seq)

    def chunks(col):
        a = p[:, col:col + wkv].reshape(bsz, seq, NSA_GROUPS, HEAD_DIM).transpose(0, 2, 1, 3)
        return a.reshape(bsz * NSA_GROUPS, seq // NSA_CMP_STRIDE, NSA_CMP_STRIDE * HEAD_DIM)

    k_cmp = _compress(chunks(wq), pe_k, w1_k, w2_k)
    v_cmp = _compress(chunks(wq + wkv), pe_v, w1_v, w2_v)
    sq = lambda a: a.reshape(bsz, seq, wq)
    skv = lambda a: a.reshape(bsz, seq, wkv)
    o_cmp, sel = _cmp_select(sq(q), k_cmp, v_cmp)
    o_sel = _sel_attention(sq(qr), skv(ks), skv(vs), sel)
    gates = gt[:, :NSA_HEADS * 3].reshape(bsz, seq, NSA_GROUPS, NSA_REP * 3)
    gates = jnp.pad(gates, ((0, 0), (0, 0), (0, 0), (0, LANES - NSA_REP * 3))).reshape(bsz, seq, NSA_GROUPS * LANES)
    o = _win_merge(sq(qr), skv(kw), skv(vw), o_cmp, o_sel, gates)
    return _proj_ln([o.reshape(-1, wq)], [w_out.astype(BF16)], h, ln_g, ln_b)


def kernel(x, a_w_in, a_lam_q1, a_lam_k1, a_lam_q2, a_lam_k2, a_subln_g, a_w_out, c_w_in, c_pe_k, c_w1_k, c_w2_k, c_pe_v, c_w1_v, c_w2_v, c_w_out, ln_mix_g, ln_mix_b, ln_ffn_g, ln_ffn_b, router_w, router_b, w_gu, w_down, ws_gu, ws_down):
    bsz, seq, d = x.shape
    h = x.reshape(bsz * seq, d)
    depth = ln_mix_g.shape[0]
    for i in range(depth):
        j = i // 2
        if i % 2 == 0:
            h = _layer_diff_moba(h, bsz, seq, a_w_in[j], a_lam_q1[j], a_lam_k1[j], a_lam_q2[j], a_lam_k2[j],
                                 a_subln_g[j], a_w_out[j], ln_mix_g[i], ln_mix_b[i], i)
        else:
            h = _layer_nsa(h, bsz, seq, c_w_in[j], c_pe_k[j], c_w1_k[j], c_w2_k[j], c_pe_v[j], c_w1_v[j],
                           c_w2_v[j], c_w_out[j], ln_mix_g[i], ln_mix_b[i])
        h = _moe_ln(h, router_w[i], router_b[i], w_gu[i], w_down[i], ws_gu[i], ws_down[i],
                    ln_ffn_g[i], ln_ffn_b[i])
    return h.reshape(bsz, seq, d)
```

```python
import functools
import math

import jax
import jax.numpy as jnp
import numpy as np
from jax import lax
from jax.experimental import pallas as pl
from jax.experimental.pallas import tpu as pltpu

F32 = jnp.float32
BF16 = jnp.bfloat16
I32 = jnp.int32

HEAD_DIM = 128
ROPE_THETA = 10000.0
NEG = -1e30
LN_EPS = 1e-5
DEPTH = 2
DN_ALPHA = (2 * DEPTH) ** 0.25

DIFF_HEADS = 8
MOBA_HEADS = 8
MOBA_BLOCK = 256
MOBA_TOPK = 3
NSA_HEADS = 16
NSA_GROUPS = 2
NSA_REP = NSA_HEADS // NSA_GROUPS
NSA_CMP_LEN = 32
NSA_CMP_STRIDE = 16
NSA_SEL_BLOCK = 64
NSA_SEL_N = 16
NSA_WINDOW = 512
NSA_FORCED = 1e4
N_EXPERTS = 64
EXPERT_DIM = 512
TOP_K = 8
N_GROUPS = 8
GROUP_SIZE = N_EXPERTS // N_GROUPS
TOPK_GROUPS = 4
ROUTED_SCALE = 2.5

LANES = 128
ISSUE_GROUP = 16
VMEM_LIMIT = 56 * 1024 * 1024

_NT = (((1,), (1,)), ((), ()))


def _cp(sem, **kw):
    return pltpu.CompilerParams(dimension_semantics=sem, vmem_limit_bytes=VMEM_LIMIT, **kw)


def _layernorm(x, g, b):
    mu = jnp.mean(x, axis=-1, keepdims=True)
    xc = x - mu
    var = jnp.mean(xc * xc, axis=-1, keepdims=True)
    return xc * lax.rsqrt(var + LN_EPS) * g + b


def _mm_kernel(x_ref, w_ref, o_ref):
    o_ref[...] = jnp.dot(x_ref[...].astype(BF16), w_ref[...],
                         preferred_element_type=F32).astype(o_ref.dtype)


def _matmul(x, w, tm, tn, out_dtype=F32):
    m, k = x.shape
    n = w.shape[1]
    tm, tn = min(tm, m), min(tn, n)
    return pl.pallas_call(
        _mm_kernel,
        grid=(m // tm, n // tn),
        in_specs=[pl.BlockSpec((tm, k), lambda i, j: (i, 0)),
                  pl.BlockSpec((k, tn), lambda i, j: (0, j))],
        out_specs=pl.BlockSpec((tm, tn), lambda i, j: (i, j)),
        out_shape=jax.ShapeDtypeStruct((m, n), out_dtype),
        compiler_params=_cp(("parallel", "arbitrary")),
        name="proj_in",
    )(x, w)


def _proj_ln_kernel(*refs, n_in, alpha):
    a_refs, w_refs = refs[:n_in], refs[n_in:2 * n_in]
    h_ref, g_ref, b_ref, o_ref = refs[2 * n_in:]
    acc = alpha * h_ref[...]
    for a_ref, w_ref in zip(a_refs, w_refs):
        acc = acc + jnp.dot(a_ref[...], w_ref[...], preferred_element_type=F32)
    o_ref[...] = _layernorm(acc, g_ref[...], b_ref[...])


def _proj_ln(acts, ws, h, g, b, tm=256):
    m, d = h.shape
    tm = min(tm, m)
    n_in = len(acts)
    in_specs = ([pl.BlockSpec((tm, a.shape[1]), lambda i: (i, 0)) for a in acts]
                + [pl.BlockSpec(w.shape, lambda i: (0, 0)) for w in ws]
                + [pl.BlockSpec((tm, d), lambda i: (i, 0)),
                   pl.BlockSpec((1, d), lambda i: (0, 0)),
                   pl.BlockSpec((1, d), lambda i: (0, 0))])
    return pl.pallas_call(
        functools.partial(_proj_ln_kernel, n_in=n_in, alpha=DN_ALPHA),
        grid=(m // tm,),
        in_specs=in_specs,
        out_specs=pl.BlockSpec((tm, d), lambda i: (i, 0)),
        out_shape=jax.ShapeDtypeStruct((m, d), F32),
        compiler_params=_cp(("parallel",)),
        name="proj_out_ln",
    )(*acts, *ws, h, g.reshape(1, d), b.reshape(1, d))


def _rope_tables(seq, d):
    half = d // 2
    inv = ROPE_THETA ** (-jnp.arange(half, dtype=F32) * 2.0 / d)
    ang = jnp.arange(seq, dtype=F32)[:, None] * inv[None, :]
    cos, sin = jnp.cos(ang), jnp.sin(ang)
    return jnp.concatenate([cos, cos], axis=1), jnp.concatenate([-sin, sin], axis=1)


def _rope128(x, cos, sin_signed):
    return x * cos + pltpu.roll(x, 64, 1) * sin_signed


def _rope64x2(x, cos, sin_lo, sin_hi):
    return x * cos + pltpu.roll(x, 96, 1) * sin_lo + pltpu.roll(x, 32, 1) * sin_hi


def _split0_kernel(p_ref, c64_ref, slo_ref, shi_ref, c128_ref, s128_ref,
                   dq_ref, dk_ref, dv_ref, mq_ref, mk_ref, mv_ref, km_ref, *, rows):
    c64, slo, shi = c64_ref[...], slo_ref[...], shi_ref[...]
    c128, s128 = c128_ref[...], s128_ref[...]
    w = DIFF_HEADS * HEAD_DIM
    dscale = (HEAD_DIM // 2) ** -0.5
    mscale = HEAD_DIM ** -0.5
    for h in range(DIFF_HEADS):
        lo, hi = h * HEAD_DIM, (h + 1) * HEAD_DIM
        dq_ref[:, lo:hi] = (_rope64x2(p_ref[:, lo:hi], c64, slo, shi) * dscale).astype(BF16)
        dk_ref[:, lo:hi] = _rope64x2(p_ref[:, w + lo:w + hi], c64, slo, shi).astype(BF16)
        dv_ref[:, lo:hi] = p_ref[:, 2 * w + lo:2 * w + hi].astype(BF16)
        mq_ref[:, lo:hi] = (_rope128(p_ref[:, 3 * w + lo:3 * w + hi], c128, s128) * mscale).astype(BF16)
        kr = _rope128(p_ref[:, 4 * w + lo:4 * w + hi], c128, s128)
        mk_ref[:, lo:hi] = kr.astype(BF16)
        km_ref[0, :, lo:hi] = jnp.sum(kr, axis=0, keepdims=True) * (1.0 / rows)
        mv_ref[:, lo:hi] = p_ref[:, 5 * w + lo:5 * w + hi].astype(BF16)


def _split0(p, seq):
    t = p.shape[0]
    rows = MOBA_BLOCK
    w = DIFF_HEADS * HEAD_DIM
    nb = seq // rows
    c64h, s64h = _rope_tables(seq, HEAD_DIM // 2)
    c64 = jnp.concatenate([c64h, c64h], axis=1)
    lane = jnp.arange(HEAD_DIM) % (HEAD_DIM // 2)
    s64 = jnp.concatenate([s64h, s64h], axis=1)
    slo = jnp.where(lane < HEAD_DIM // 4, s64, 0.0)
    shi = jnp.where(lane >= HEAD_DIM // 4, s64, 0.0)
    c128, s128 = _rope_tables(seq, HEAD_DIM)
    tab = pl.BlockSpec((rows, HEAD_DIM), lambda i: (i % nb, 0))
    out = pl.BlockSpec((rows, w), lambda i: (i, 0))
    bf = jax.ShapeDtypeStruct((t, w), BF16)
    return pl.pallas_call(
        functools.partial(_split0_kernel, rows=rows),
        grid=(t // rows,),
        in_specs=[pl.BlockSpec((rows, 6 * w), lambda i: (i, 0)), tab, tab, tab, tab, tab],
        out_specs=[out] * 6 + [pl.BlockSpec((1, 1, w), lambda i: (i, 0, 0))],
        out_shape=[bf] * 6 + [jax.ShapeDtypeStruct((t // rows, 1, w), F32)],
        compiler_params=_cp(("parallel",)),
        name="split_rope_l0",
    )(p, c64, slo, shi, c128, s128)


def _flash_update(carry, s, v):
    m, l, acc = carry
    m_new = jnp.maximum(m, jnp.max(s, axis=1, keepdims=True))
    p = jnp.exp(s - m_new)
    a = jnp.exp(m - m_new)
    l = a * l + jnp.sum(p, axis=1, keepdims=True)
    acc = a * acc + jnp.dot(p.astype(BF16), v, preferred_element_type=F32)
    return m_new, l, acc


def _flash_init(rows, d):
    return (jnp.full((rows, 1), NEG, F32), jnp.zeros((rows, 1), F32), jnp.zeros((rows, d), F32))


def _diff_kernel(lam_ref, q_ref, k_ref, v_ref, g_ref, o_ref, *, tq, out_scale):
    i = pl.program_id(2)
    q = q_ref[0]
    lane = lax.broadcasted_iota(I32, (tq, HEAD_DIM), 1)
    zero = jnp.zeros_like(q)
    qs = jnp.concatenate([jnp.where(lane < HEAD_DIM // 2, q, zero),
                          jnp.where(lane >= HEAD_DIM // 2, q, zero)], axis=0)

    def tile(j):
        st = pl.multiple_of(j * tq, tq)
        k = k_ref[0, pl.ds(st, tq), :]
        v = v_ref[0, pl.ds(st, tq), :]
        return lax.dot_general(qs, k, _NT, preferred_element_type=F32), v

    def body(j, carry):
        s, v = tile(j)
        return _flash_update(carry, s, v)

    carry = lax.fori_loop(0, i, body, _flash_init(2 * tq, HEAD_DIM))
    s, v = tile(i)
    r = lax.broadcasted_iota(I32, (2 * tq, tq), 0)
    c = lax.broadcasted_iota(I32, (2 * tq, tq), 1)
    s = jnp.where(c <= jnp.where(r >= tq, r - tq, r), s, NEG)
    _, l, acc = _flash_update(carry, s, v)
    o = acc / l
    od = o[:tq] - lam_ref[0] * o[tq:]
    od = od * lax.rsqrt(jnp.mean(od * od, axis=1, keepdims=True) + LN_EPS) * g_ref[...] * out_scale
    o_ref[0] = od.astype(o_ref.dtype)


def _diff_attention(lam, q, k, v, subln_g, lam_init, tq=512):
    b, s, _ = q.shape
    tq = min(tq, s)
    kv = pl.BlockSpec((1, s, HEAD_DIM), lambda bi, h, i, *_: (bi, 0, h))
    qo = pl.BlockSpec((1, tq, HEAD_DIM), lambda bi, h, i, *_: (bi, i, h))
    return pl.pallas_call(
        functools.partial(_diff_kernel, tq=tq, out_scale=1.0 - lam_init),
        grid_spec=pltpu.PrefetchScalarGridSpec(
            num_scalar_prefetch=1,
            grid=(b, DIFF_HEADS, s // tq),
            in_specs=[qo, kv, kv, pl.BlockSpec((1, HEAD_DIM), lambda bi, h, i, *_: (0, 0))],
            out_specs=qo),
        out_shape=jax.ShapeDtypeStruct(q.shape, BF16),
        compiler_params=_cp(("parallel", "parallel", "arbitrary")),
        name="diff_attention",
    )(lam, q, k, v, subln_g.reshape(1, HEAD_DIM))


def _moba_kernel(q_ref, k_ref, v_ref, km_ref, o_ref, *, tq, blk, nb, n_pick, hp):
    i = pl.program_id(2)
    bid = lax.broadcasted_iota(I32, (tq, nb), 1)
    own = (i * tq + lax.broadcasted_iota(I32, (tq, nb), 0)) // blk
    past = bid < own
    own_col = (i * tq + lax.broadcasted_iota(I32, (tq, 1), 0)) // blk
    rpos = lax.broadcasted_iota(I32, (tq, blk), 0) & (blk - 1)
    kcol = lax.broadcasted_iota(I32, (tq, blk), 1)
    causal = kcol <= rpos
    heads = [slice(h * HEAD_DIM, (h + 1) * HEAD_DIM) for h in range(hp)]

    qs, sels = [], []
    for hs in heads:
        q = q_ref[0, :, hs]
        gate = lax.dot_general(q.astype(F32), km_ref[0, :, hs], _NT, precision=lax.Precision.HIGHEST,
                               preferred_element_type=F32)
        g = jnp.where(past, gate, NEG)
        sel = jnp.zeros((tq, nb), F32)
        for _ in range(n_pick):
            mx = jnp.max(g, axis=1, keepdims=True)
            idx = jnp.min(jnp.where(g == mx, bid, nb), axis=1, keepdims=True)
            pick = bid == idx
            sel = jnp.where(pick, jnp.where(past, 1.0, 0.0), sel)
            g = jnp.where(pick, -jnp.inf, g)
        qs.append(q)
        sels.append(sel)

    def body(j, carries):
        st = pl.multiple_of(j * blk, blk)
        out = []
        for hs, q, sel, carry in zip(heads, qs, sels, carries):
            s = lax.dot_general(q, k_ref[0, pl.ds(st, blk), hs], _NT, preferred_element_type=F32)
            picked = jnp.sum(jnp.where(bid == j, sel, 0.0), axis=1, keepdims=True) > 0.5
            keep = picked | ((own_col == j) & causal)
            out.append(_flash_update(carry, jnp.where(keep, s, NEG), v_ref[0, pl.ds(st, blk), hs]))
        return tuple(out)

    n_blocks = (i + 1) * (tq // blk)
    carries = lax.fori_loop(0, n_blocks, body, tuple(_flash_init(tq, HEAD_DIM) for _ in heads))
    for hs, (_, l, acc) in zip(heads, carries):
        o_ref[0, :, hs] = (acc / l).astype(o_ref.dtype)


def _moba_attention(q, k, v, kmean, tq=1024, hp=2):
    b, s, _ = q.shape
    blk = min(MOBA_BLOCK, s)
    tq = min(tq, s)
    nb = s // blk
    n_pick = max(1, min(MOBA_TOPK, nb - 1))
    kv = pl.BlockSpec((1, s, hp * HEAD_DIM), lambda bi, h, i: (bi, 0, h))
    qo = pl.BlockSpec((1, tq, hp * HEAD_DIM), lambda bi, h, i: (bi, i, h))
    return pl.pallas_call(
        functools.partial(_moba_kernel, tq=tq, blk=blk, nb=nb, n_pick=n_pick, hp=hp),
        grid=(b, MOBA_HEADS // hp, s // tq),
        in_specs=[qo, kv, kv, pl.BlockSpec((1, nb, hp * HEAD_DIM), lambda bi, h, i: (bi, 0, h))],
        out_specs=qo,
        out_shape=jax.ShapeDtypeStruct(q.shape, BF16),
        compiler_params=_cp(("parallel", "parallel", "arbitrary")),
        name="moba_attention",
    )(q, k, v, kmean)


def _split1_kernel(p_ref, c_ref, s_ref, q_ref, qr_ref, ks_ref, vs_ref, kw_ref, vw_ref, gt_ref):
    cos, sin = c_ref[...], s_ref[...]
    scale = HEAD_DIM ** -0.5
    wq = NSA_HEADS * HEAD_DIM
    wkv = NSA_GROUPS * HEAD_DIM
    for h in range(NSA_HEADS):
        lo, hi = h * HEAD_DIM, (h + 1) * HEAD_DIM
        x = p_ref[:, lo:hi]
        q_ref[:, lo:hi] = (x * scale).astype(BF16)
        qr_ref[:, lo:hi] = (_rope128(x, cos, sin) * scale).astype(BF16)
    base = wq + 2 * wkv
    for g in range(NSA_GROUPS):
        lo, hi = g * HEAD_DIM, (g + 1) * HEAD_DIM
        ks_ref[:, lo:hi] = _rope128(p_ref[:, base + lo:base + hi], cos, sin).astype(BF16)
        vs_ref[:, lo:hi] = p_ref[:, base + wkv + lo:base + wkv + hi].astype(BF16)
        kw_ref[:, lo:hi] = _rope128(p_ref[:, base + 2 * wkv + lo:base + 2 * wkv + hi], cos, sin).astype(BF16)
        vw_ref[:, lo:hi] = p_ref[:, base + 3 * wkv + lo:base + 3 * wkv + hi].astype(BF16)
    gt_ref[...] = jax.nn.sigmoid(p_ref[:, base + 4 * wkv:base + 4 * wkv + LANES])


def _split1(p, seq, rows=256):
    t, wp = p.shape
    rows = min(rows, seq)
    nb = seq // rows
    wq = NSA_HEADS * HEAD_DIM
    wkv = NSA_GROUPS * HEAD_DIM
    cos, sin = _rope_tables(seq, HEAD_DIM)
    tab = pl.BlockSpec((rows, HEAD_DIM), lambda i: (i % nb, 0))
    oq = pl.BlockSpec((rows, wq), lambda i: (i, 0))
    okv = pl.BlockSpec((rows, wkv), lambda i: (i, 0))
    sq = jax.ShapeDtypeStruct((t, wq), BF16)
    skv = jax.ShapeDtypeStruct((t, wkv), BF16)
    return pl.pallas_call(
        _split1_kernel,
        grid=(t // rows,),
        in_specs=[pl.BlockSpec((rows, wp), lambda i: (i, 0)), tab, tab],
        out_specs=[oq, oq, okv, okv, okv, okv, pl.BlockSpec((rows, LANES), lambda i: (i, 0))],
        out_shape=[sq, sq, skv, skv, skv, skv, jax.ShapeDtypeStruct((t, LANES), F32)],
        compiler_params=_cp(("parallel",)),
        name="split_rope_l1",
    )(p, cos, sin)


def _gelu_tanh(x):
    return 0.5 * x * (1.0 + jnp.tanh(math.sqrt(2.0 / math.pi) * (x + 0.044715 * (x * x * x))))


def _compress_kernel(c_ref, pelo_ref, pehi_ref, w1lo_ref, w1hi_ref, w2_ref, o_ref, *, nc):
    c = c_ref[0]
    a = jnp.dot((c + pelo_ref[...]).astype(BF16), w1lo_ref[...], preferred_element_type=F32)
    bm = jnp.dot((c + pehi_ref[...]).astype(BF16), w1hi_ref[...], preferred_element_type=F32)
    hid = _gelu_tanh(a + pltpu.roll(bm, nc - 1, 0))
    o_ref[0] = jnp.dot(hid.astype(BF16), w2_ref[...], preferred_element_type=F32).astype(o_ref.dtype)


def _compress(chunks, pe, w1, w2):
    bg, nc, cw = chunks.shape
    half = NSA_CMP_STRIDE
    pelo = pe[:half].reshape(1, cw)
    pehi = pe[half:].reshape(1, cw)
    w1lo = w1[:cw].astype(BF16)
    w1hi = w1[cw:].astype(BF16)
    full = lambda a: pl.BlockSpec(a.shape, lambda i: (0,) * a.ndim)
    return pl.pallas_call(
        functools.partial(_compress_kernel, nc=nc),
        grid=(bg,),
        in_specs=[pl.BlockSpec((1, nc, cw), lambda i: (i, 0, 0)), full(pelo), full(pehi),
                  full(w1lo), full(w1hi), pl.BlockSpec(w2.shape, lambda i: (0, 0))],
        out_specs=pl.BlockSpec((1, nc, HEAD_DIM), lambda i: (i, 0, 0)),
        out_shape=jax.ShapeDtypeStruct((bg, nc, HEAD_DIM), BF16),
        compiler_params=_cp(("parallel",)),
        name="nsa_compress",
    )(chunks, pelo, pehi, w1lo, w1hi, w2.astype(BF16))


def _stack_heads(q):
    return jnp.concatenate([q[:, r * HEAD_DIM:(r + 1) * HEAD_DIM] for r in range(NSA_REP)], axis=0)


def _unstack_store(o_ref, o, tq):
    for r in range(NSA_REP):
        o_ref[0, :, r * HEAD_DIM:(r + 1) * HEAD_DIM] = o[r * tq:(r + 1) * tq].astype(o_ref.dtype)


def _cmp_select_kernel(q_ref, kc_ref, vc_ref, o_ref, sel_ref, *, tq, nc, nblk, n_pick):
    i = pl.program_id(2)
    qs = _stack_heads(q_ref[0])
    s = lax.dot_general(qs, kc_ref[0], _NT, preferred_element_type=F32)
    rows = NSA_REP * tq
    row = lax.broadcasted_iota(I32, (rows, nc), 0)
    n = lax.broadcasted_iota(I32, (rows, nc), 1)
    pos = i * tq + (row & (tq - 1))
    valid = n * NSA_CMP_STRIDE + (NSA_CMP_LEN - 1) <= pos
    sm = jnp.where(valid, s, NEG)
    e = jnp.exp(sm - jnp.max(sm, axis=1, keepdims=True))
    p = jnp.where(valid, e / jnp.sum(e, axis=1, keepdims=True), 0.0)
    o = jnp.dot(p.astype(BF16), vc_ref[0], preferred_element_type=F32)
    _unstack_store(o_ref, o, tq)

    psum = p[0:tq]
    for r in range(1, NSA_REP):
        psum = psum + p[r * tq:(r + 1) * tq]
    cn = lax.broadcasted_iota(I32, (nc, nblk), 0) * NSA_CMP_STRIDE
    bj = lax.broadcasted_iota(I32, (nc, nblk), 1) * NSA_SEL_BLOCK
    ov = jnp.maximum(jnp.minimum(cn + NSA_CMP_LEN, bj + NSA_SEL_BLOCK) - jnp.maximum(cn, bj), 0)
    ov = ov.astype(F32) * (1.0 / NSA_CMP_LEN)
    imp = jnp.dot(psum, ov, precision=lax.Precision.HIGHEST, preferred_element_type=F32)

    cur = (i * tq + lax.broadcasted_iota(I32, (tq, nblk), 0)) // NSA_SEL_BLOCK
    blk = lax.broadcasted_iota(I32, (tq, nblk), 1)
    allowed = blk <= cur
    forced = (blk == 0) | (blk == cur) | (blk == cur - 1)
    val = jnp.where(allowed, jnp.where(forced, NSA_FORCED, imp), NEG)
    sel = jnp.zeros((tq, nblk), F32)
    for _ in range(n_pick):
        mx = jnp.max(val, axis=1, keepdims=True)
        idx = jnp.min(jnp.where(val == mx, blk, nblk), axis=1, keepdims=True)
        pick = blk == idx
        sel = jnp.where(pick, 1.0, sel)
        val = jnp.where(pick, -jnp.inf, val)
    sel_ref[0, 0] = jnp.where(allowed, sel, 0.0)


def _cmp_select(q, kcmp, vcmp, tq=256):
    b, s, _ = q.shape
    tq = min(tq, s)
    nc = kcmp.shape[1]
    nblk = s // NSA_SEL_BLOCK
    n_pick = min(NSA_SEL_N, nblk)
    wg = NSA_REP * HEAD_DIM
    qo = pl.BlockSpec((1, tq, wg), lambda bi, g, i: (bi, i, g))
    kv = pl.BlockSpec((1, nc, HEAD_DIM), lambda bi, g, i: (bi * NSA_GROUPS + g, 0, 0))
    return pl.pallas_call(
        functools.partial(_cmp_select_kernel, tq=tq, nc=nc, nblk=nblk, n_pick=n_pick),
        grid=(b, NSA_GROUPS, s // tq),
        in_specs=[qo, kv, kv],
        out_specs=[qo, pl.BlockSpec((1, 1, tq, nblk), lambda bi, g, i: (bi, g, i, 0))],
        out_shape=[jax.ShapeDtypeStruct(q.shape, F32),
                   jax.ShapeDtypeStruct((b, NSA_GROUPS, s, nblk), F32)],
        compiler_params=_cp(("parallel", "parallel", "arbitrary")),
        name="nsa_cmp_select",
    )(q, kcmp, vcmp)


def _mask_heads(s, keep, tq):
    tk = s.shape[1]
    s3 = jnp.where(keep[None], s.reshape(NSA_REP, tq, tk), NEG)
    return s3.reshape(NSA_REP * tq, tk)


def _sel_kernel(q_ref, k_ref, v_ref, sel_ref, o_ref, *, tq, tk, nblk):
    i = pl.program_id(2)
    qs = _stack_heads(q_ref[0])
    sel = sel_ref[0, 0].astype(BF16)
    per = tk // NSA_SEL_BLOCK
    qpos = i * tq + lax.broadcasted_iota(I32, (tq, tk), 0)
    kcol = lax.broadcasted_iota(I32, (tq, tk), 1)
    eb = lax.broadcasted_iota(I32, (nblk, tk), 0)
    ec = lax.broadcasted_iota(I32, (nblk, tk), 1) // NSA_SEL_BLOCK

    def body(j, carry):
        st = pl.multiple_of(j * tk, tk)
        k = k_ref[0, pl.ds(st, tk), :]
        v = v_ref[0, pl.ds(st, tk), :]
        s = lax.dot_general(qs, k, _NT, preferred_element_type=F32)
        expand = jnp.where(eb == j * per + ec, 1.0, 0.0).astype(BF16)
        on = jnp.dot(sel, expand, preferred_element_type=F32)
        keep = jnp.where(j * tk + kcol <= qpos, on, 0.0) > 0.5
        return _flash_update(carry, _mask_heads(s, keep, tq), v)

    n_tiles = (i * tq + tq - 1) // tk + 1
    _, l, acc = lax.fori_loop(0, n_tiles, body, _flash_init(NSA_REP * tq, HEAD_DIM))
    _unstack_store(o_ref, acc / l, tq)


def _sel_attention(q, k, v, sel, tq=128, tk=512):
    b, s, _ = q.shape
    tq, tk = min(tq, s), min(tk, s)
    nblk = s // NSA_SEL_BLOCK
    wg = NSA_REP * HEAD_DIM
    qo = pl.BlockSpec((1, tq, wg), lambda bi, g, i: (bi, i, g))
    kv = pl.BlockSpec((1, s, HEAD_DIM), lambda bi, g, i: (bi, 0, g))
    return pl.pallas_call(
        functools.partial(_sel_kernel, tq=tq, tk=tk, nblk=nblk),
        grid=(b, NSA_GROUPS, s // tq),
        in_specs=[qo, kv, kv, pl.BlockSpec((1, 1, tq, nblk), lambda bi, g, i: (bi, g, i, 0))],
        out_specs=qo,
        out_shape=jax.ShapeDtypeStruct(q.shape, F32),
        compiler_params=_cp(("parallel", "parallel", "arbitrary")),
        name="nsa_sel_attention",
    )(q, k, v, sel)


def _win_kernel(q_ref, k_ref, v_ref, oc_ref, os_ref, gt_ref, o_ref, *, tq, window):
    i = pl.program_id(2)
    qs = _stack_heads(q_ref[0])
    qpos = i * tq + lax.broadcasted_iota(I32, (tq, tq), 0)
    kcol = lax.broadcasted_iota(I32, (tq, tq), 1)

    def body(j, carry):
        st = pl.multiple_of(j * tq, tq)
        k = k_ref[0, pl.ds(st, tq), :]
        v = v_ref[0, pl.ds(st, tq), :]
        s = lax.dot_general(qs, k, _NT, preferred_element_type=F32)
        kpos = j * tq + kcol
        keep = (kpos <= qpos) & (kpos > qpos - window)
        return _flash_update(carry, _mask_heads(s, keep, tq), v)

    def rev_body(t, carry):
        return body(i - t, carry)

    n_tiles = jnp.minimum(i, window // tq) + 1
    _, l, acc = lax.fori_loop(0, n_tiles, rev_body, _flash_init(NSA_REP * tq, HEAD_DIM))
    ow = acc / l
    gt = gt_ref[0]
    for r in range(NSA_REP):
        lo, hi = r * HEAD_DIM, (r + 1) * HEAD_DIM
        mix = (gt[:, 3 * r:3 * r + 1] * oc_ref[0, :, lo:hi]
               + gt[:, 3 * r + 1:3 * r + 2] * os_ref[0, :, lo:hi]
               + gt[:, 3 * r + 2:3 * r + 3] * ow[r * tq:(r + 1) * tq])
        o_ref[0, :, lo:hi] = mix.astype(o_ref.dtype)


def _win_merge(q, k, v, o_cmp, o_sel, gates, tq=256):
    b, s, _ = q.shape
    tq = min(tq, s)
    wg = NSA_REP * HEAD_DIM
    qo = pl.BlockSpec((1, tq, wg), lambda bi, g, i: (bi, i, g))
    kv = pl.BlockSpec((1, s, HEAD_DIM), lambda bi, g, i: (bi, 0, g))
    return pl.pallas_call(
        functools.partial(_win_kernel, tq=tq, window=NSA_WINDOW),
        grid=(b, NSA_GROUPS, s // tq),
        in_specs=[qo, kv, kv, qo, qo, pl.BlockSpec((1, tq, LANES), lambda bi, g, i: (bi, i, g))],
        out_specs=qo,
        out_shape=jax.ShapeDtypeStruct(q.shape, BF16),
        compiler_params=_cp(("parallel", "parallel", "arbitrary")),
        name="nsa_window_merge",
    )(q, k, v, o_cmp, o_sel, gates)


def _router_kernel(x_ref, w_ref, b_ref, idx_ref, wt_ref):
    tm = x_ref.shape[0]
    logits = jnp.dot(x_ref[...], w_ref[...], precision=lax.Precision.HIGHEST,
                     preferred_element_type=F32)
    score = jax.nn.sigmoid(logits)
    biased = score + b_ref[...]
    lane = lax.broadcasted_iota(I32, (tm, N_EXPERTS), 1)
    grp = lane // GROUP_SIZE

    gscore = jnp.zeros((tm, N_EXPERTS), F32)
    for g in range(N_GROUPS):
        ing = grp == g
        v = jnp.where(ing, biased, -jnp.inf)
        m1 = jnp.max(v, axis=1, keepdims=True)
        i1 = jnp.min(jnp.where(v == m1, lane, N_EXPERTS), axis=1, keepdims=True)
        m2 = jnp.max(jnp.where(lane == i1, -jnp.inf, v), axis=1, keepdims=True)
        gscore = jnp.where(ing, m1 + m2, gscore)

    emask = jnp.zeros((tm, N_EXPERTS), F32)
    for _ in range(TOPK_GROUPS):
        mx = jnp.max(gscore, axis=1, keepdims=True)
        gi = jnp.min(jnp.where(gscore == mx, grp, N_GROUPS), axis=1, keepdims=True)
        pick = grp == gi
        emask = jnp.where(pick, 1.0, emask)
        gscore = jnp.where(pick, -jnp.inf, gscore)

    val = jnp.where(emask > 0.5, biased, NEG)
    kcol = lax.broadcasted_iota(I32, (tm, TOP_K), 1)
    idx_out = jnp.zeros((tm, TOP_K), I32)
    wt_out = jnp.zeros((tm, TOP_K), F32)
    for kk in range(TOP_K):
        mx = jnp.max(val, axis=1, keepdims=True)
        ei = jnp.min(jnp.where(val == mx, lane, N_EXPERTS), axis=1, keepdims=True)
        pick = lane == ei
        wk = jnp.sum(jnp.where(pick, score, 0.0), axis=1, keepdims=True)
        idx_out = jnp.where(kcol == kk, ei, idx_out)
        wt_out = jnp.where(kcol == kk, wk, wt_out)
        val = jnp.where(pick, -jnp.inf, val)
    idx_ref[...] = idx_out
    wt_ref[...] = wt_out / jnp.sum(wt_out, axis=1, keepdims=True) * ROUTED_SCALE


def _router(x, w, bias, tm=256):
    t, d = x.shape
    tm = min(tm, t)
    return pl.pallas_call(
        _router_kernel,
        grid=(t // tm,),
        in_specs=[pl.BlockSpec((tm, d), lambda i: (i, 0)),
                  pl.BlockSpec((d, N_EXPERTS), lambda i: (0, 0)),
                  pl.BlockSpec((1, N_EXPERTS), lambda i: (0, 0))],
        out_specs=[pl.BlockSpec((tm, TOP_K), lambda i: (i, 0)),
                   pl.BlockSpec((tm, TOP_K), lambda i: (i, 0))],
        out_shape=[jax.ShapeDtypeStruct((t, TOP_K), I32), jax.ShapeDtypeStruct((t, TOP_K), F32)],
        compiler_params=_cp(("parallel",)),
        name="moe_router",
    )(x, w, bias.reshape(1, N_EXPERTS))


def _moe_plan(eidx, tm):
    t = eidx.shape[0]
    n_slots = t * TOP_K
    flat_e = eidx.reshape(-1)
    order = jnp.argsort(flat_e, stable=True).astype(I32)
    rank = jnp.argsort(order).astype(I32)
    experts = jnp.arange(N_EXPERTS, dtype=I32)
    counts = jnp.sum((flat_e[:, None] == experts[None, :]).astype(I32), axis=0)
    pcounts = (counts + tm - 1) // tm * tm
    pend = jnp.cumsum(pcounts)
    pstart = pend - pcounts
    cstart = jnp.cumsum(counts) - counts
    n_rows = n_slots + N_EXPERTS * tm
    n_tiles = n_rows // tm
    tile_start = jnp.arange(n_tiles, dtype=I32) * tm
    tile_e = jnp.clip(jnp.sum((pend[None, :] <= tile_start[:, None]).astype(I32), axis=1), 0, N_EXPERTS - 1)
    j = tile_start[:, None] + jnp.arange(tm, dtype=I32)[None, :] - pstart[tile_e][:, None]
    src = jnp.clip(cstart[tile_e][:, None] + j, 0, n_slots - 1)
    row_tok = jnp.where(j < counts[tile_e][:, None], order[src] // TOP_K, 0).astype(I32)
    slot_row = ((pstart - cstart)[flat_e] + rank).astype(I32)
    n_used = (pend[-1] // tm).astype(I32).reshape(1)
    return row_tok, slot_row, tile_e.astype(I32), n_used


def _gather_rows_pipelined(i, n_steps, idx_hbm, src_hbm, idx_smem, buf, idx_sem, row_sem, rows):
    slot = i % 2

    def idx_copy(t, s):
        return pltpu.make_async_copy(idx_hbm.at[t], idx_smem[s], idx_sem.at[s])

    def issue_rows(s):
        def issue(g, c):
            for k in range(ISSUE_GROUP):
                pltpu.make_async_copy(src_hbm.at[pl.ds(idx_smem[s][g * ISSUE_GROUP + k], 1)],
                                      buf.at[s, g * (ISSUE_GROUP // 8) + k // 8, pl.ds(k % 8, 1)],
                                      row_sem.at[s]).start()
            return c
        lax.fori_loop(0, rows // ISSUE_GROUP, issue, 0)

    @pl.when(i == 0)
    def _():
        idx_copy(0, 0).start()
        idx_copy(0, 0).wait()
        issue_rows(0)

        @pl.when(n_steps > 1)
        def _():
            idx_copy(1, 1).start()

    for s in range(2):
        @pl.when(jnp.logical_and(i + 1 < n_steps, slot == 1 - s))
        def _():
            idx_copy(i + 1, s).wait()
            issue_rows(s)

            @pl.when(i + 2 < n_steps)
            def _():
                idx_copy(i + 2, 1 - s).start()

    @pl.when(i < n_steps)
    def _():
        pltpu.make_async_copy(buf.at[slot], buf.at[slot], row_sem.at[slot]).wait()


def _expert_kernel(te_ref, nused_ref, tok_hbm, x_hbm, wgu_ref, wd_ref, o_ref,
                   idx0, idx1, xbuf, idx_sem, row_sem, *, tm):
    i = pl.program_id(0)
    n_used = nused_ref[0]
    d = x_hbm.shape[1]
    idx_smem = (idx0, idx1)
    slot = i % 2
    n_chunks = 8
    per = tm // n_chunks

    def idx_copy(t, s):
        return pltpu.make_async_copy(tok_hbm.at[t], idx_smem[s], idx_sem.at[s])

    def row_start(s, g, k):
        pltpu.make_async_copy(x_hbm.at[pl.ds(idx_smem[s][g * 8 + k], 1)], xbuf.at[s, g, pl.ds(k, 1)],
                              row_sem.at[s]).start()

    @pl.when(i == 0)
    def _():
        idx_copy(0, 0).start()
        idx_copy(0, 0).wait()

        def issue(g, c):
            for k in range(8):
                row_start(0, g, k)
            return c
        lax.fori_loop(0, tm // 8, issue, 0)
        idx_copy(1, 1).start()

    @pl.when(i <= n_used)
    def _():
        pltpu.make_async_copy(xbuf.at[slot], xbuf.at[slot], row_sem.at[slot]).wait()

    for s in range(2):
        @pl.when(jnp.logical_and(i < n_used, slot == s))
        def _():
            idx_copy(i + 1, 1 - s).wait()
            x = xbuf[s].reshape(tm, d).astype(BF16)
            wgu = wgu_ref[0]
            wd = wd_ref[0]
            nq = n_chunks // 2
            cw = 2 * EXPERT_DIM // nq
            gus = []
            for c in range(nq):
                gus.append(jnp.dot(x, wgu[:, c * cw:(c + 1) * cw], preferred_element_type=F32))
                for n in range(c * per, (c + 1) * per):
                    row_start(1 - s, n // 8, n % 8)
            gu = jnp.concatenate(gus, axis=1)
            act = (jax.nn.silu(gu[:, :EXPERT_DIM]) * gu[:, EXPERT_DIM:]).astype(BF16)
            ow = d // nq
            for c in range(nq):
                o_ref[:, c * ow:(c + 1) * ow] = jnp.dot(act, wd[:, c * ow:(c + 1) * ow], preferred_element_type=F32)
                for n in range((nq + c) * per, (nq + c + 1) * per):
                    row_start(1 - s, n // 8, n % 8)

            @pl.when(i + 2 <= n_used)
            def _():
                idx_copy(i + 2, s).start()

    @pl.when(i >= n_used)
    def _():
        o_ref[...] = jnp.zeros_like(o_ref)


def _expert_ffn(x, row_tok, w_gu, w_down, tile_e, n_used):
    n_tiles, tm = row_tok.shape
    d = x.shape[1]
    last = lambda i, te, nu: jnp.minimum(i, nu[0] - 1)
    return pl.pallas_call(
        functools.partial(_expert_kernel, tm=tm),
        grid_spec=pltpu.PrefetchScalarGridSpec(
            num_scalar_prefetch=2,
            grid=(n_tiles,),
            in_specs=[pl.BlockSpec(memory_space=pl.ANY), pl.BlockSpec(memory_space=pl.ANY),
                      pl.BlockSpec((1, d, 2 * EXPERT_DIM), lambda i, te, nu: (te[last(i, te, nu)], 0, 0)),
                      pl.BlockSpec((1, EXPERT_DIM, d), lambda i, te, nu: (te[last(i, te, nu)], 0, 0))],
            out_specs=pl.BlockSpec((tm, d), lambda i, te, nu: (i, 0)),
            scratch_shapes=[pltpu.SMEM((tm,), I32), pltpu.SMEM((tm,), I32), pltpu.VMEM((2, tm // 8, 8, d), F32),
                            pltpu.SemaphoreType.DMA((2,)), pltpu.SemaphoreType.DMA((2,))]),
        out_shape=jax.ShapeDtypeStruct((n_tiles * tm, d), F32),
        compiler_params=_cp(("arbitrary",)),
        name="moe_expert_ffn",
    )(tile_e, n_used, row_tok, x, w_gu, w_down)


def _shared_kernel(x_ref, wgu_ref, wd_ref, o_ref):
    gu = jnp.dot(x_ref[...].astype(BF16), wgu_ref[...], preferred_element_type=F32)
    act = jax.nn.silu(gu[:, :EXPERT_DIM]) * gu[:, EXPERT_DIM:]
    o_ref[...] = jnp.dot(act.astype(BF16), wd_ref[...], preferred_element_type=F32)


def _shared_ffn(x, w_gu, w_down, tm=512):
    t, d = x.shape
    tm = min(tm, t)
    return pl.pallas_call(
        _shared_kernel,
        grid=(t // tm,),
        in_specs=[pl.BlockSpec((tm, d), lambda i: (i, 0)),
                  pl.BlockSpec(w_gu.shape, lambda i: (0, 0)),
                  pl.BlockSpec(w_down.shape, lambda i: (0, 0))],
        out_specs=pl.BlockSpec((tm, d), lambda i: (i, 0)),
        out_shape=jax.ShapeDtypeStruct((t, d), F32),
        compiler_params=_cp(("parallel",)),
        name="moe_shared_ffn",
    )(x, w_gu, w_down)


def _combine_kernel(rows_hbm, ys_hbm, wt_ref, h_ref, sh_ref, g_ref, b_ref, o_ref,
                    idx0, idx1, buf, idx_sem, row_sem, *, tt, steps):
    i = pl.program_id(0)
    d = h_ref.shape[1]
    _gather_rows_pipelined(i, steps, rows_hbm, ys_hbm, (idx0, idx1), buf, idx_sem, row_sem, tt * TOP_K)
    wt = wt_ref[...]
    acc = DN_ALPHA * h_ref[...] + sh_ref[...]
    for k in range(TOP_K):
        yk = buf[i % 2, k * (tt // 8):(k + 1) * (tt // 8)].reshape(tt, d)
        acc = acc + wt[:, k:k + 1] * yk
    o_ref[...] = _layernorm(acc, g_ref[...], b_ref[...])


def _combine_ln(ys, slot_row, wts, h, shared, g, b, tt=64):
    t, d = h.shape
    tt = min(tt, t)
    steps = t // tt
    blk = pl.BlockSpec((tt, d), lambda i: (i, 0))
    vec = pl.BlockSpec((1, d), lambda i: (0, 0))
    return pl.pallas_call(
        functools.partial(_combine_kernel, tt=tt, steps=steps),
        grid=(steps,),
        in_specs=[pl.BlockSpec(memory_space=pl.ANY), pl.BlockSpec(memory_space=pl.ANY),
                  pl.BlockSpec((tt, TOP_K), lambda i: (i, 0)), blk, blk, vec, vec],
        out_specs=blk,
        out_shape=jax.ShapeDtypeStruct((t, d), F32),
        scratch_shapes=[pltpu.SMEM((tt * TOP_K,), I32), pltpu.SMEM((tt * TOP_K,), I32),
                        pltpu.VMEM((2, tt * TOP_K // 8, 8, d), F32),
                        pltpu.SemaphoreType.DMA((2,)), pltpu.SemaphoreType.DMA((2,))],
        compiler_params=_cp(("arbitrary",)),
        name="moe_combine_ln",
    )(slot_row.reshape(steps, tt, TOP_K).transpose(0, 2, 1).reshape(steps, tt * TOP_K),
      ys, wts, h, shared, g.reshape(1, d), b.reshape(1, d))


def _moe_ln(h, router_w, router_b, w_gu, w_down, ws_gu, ws_down, g, b, tm=512):
    eidx, wts = _router(h, router_w, router_b)
    row_tok, slot_row, tile_e, n_used = _moe_plan(eidx, tm)
    ys = _expert_ffn(h, row_tok, w_gu.astype(BF16), w_down.astype(BF16), tile_e, n_used)
    shared = _shared_ffn(h, ws_gu.astype(BF16), ws_down.astype(BF16))
    return _combine_ln(ys, slot_row, wts, h, shared, g, b)


def _layer_diff_moba(h, bsz, seq, w_in, lam_q1, lam_k1, lam_q2, lam_k2, subln_g, w_out, ln_g, ln_b, layer_idx):
    w = DIFF_HEADS * HEAD_DIM
    p = _matmul(h, w_in.astype(BF16), 1024, 768)
    dq, dk, dv, mq, mk, mv, kmean = _split0(p, seq)
    lam_init = 0.8 - 0.6 * math.exp(-0.3 * layer_idx)
    lam = (jnp.exp(jnp.sum(lam_q1 * lam_k1)) - jnp.exp(jnp.sum(lam_q2 * lam_k2)) + lam_init).reshape(1)
    sh = lambda a: a.reshape(bsz, seq, w)
    o_a = _diff_attention(lam, sh(dq), sh(dk), sh(dv), subln_g, lam_init)
    o_b = _moba_attention(sh(mq), sh(mk), sh(mv), kmean.reshape(bsz, seq // MOBA_BLOCK, w))
    w_out = w_out.astype(BF16)
    return _proj_ln([o_a.reshape(-1, w), o_b.reshape(-1, w)], [w_out[:w], w_out[w:]], h, ln_g, ln_b)


def _layer_nsa(h, bsz, seq, w_in, pe_k, w1_k, w2_k, pe_v, w1_v, w2_v, w_out, ln_g, ln_b):
    wq = NSA_HEADS * HEAD_DIM
    wkv = NSA_GROUPS * HEAD_DIM
    n_in = w_in.shape[1]
    n_pad = -(-(n_in + LANES) // 768) * 768
    p = _matmul(h, jnp.pad(w_in, ((0, 0), (0, n_pad - n_in))).astype(BF16), 1024, 768)
    q, qr, ks, vs, kw, vw, gt = _split1(p, seq)

    def chunks(col):
        a = p[:, col:col + wkv].reshape(bsz, seq, NSA_GROUPS, HEAD_DIM).transpose(0, 2, 1, 3)
        return a.reshape(bsz * NSA_GROUPS, seq // NSA_CMP_STRIDE, NSA_CMP_STRIDE * HEAD_DIM)

    k_cmp = _compress(chunks(wq), pe_k, w1_k, w2_k)
    v_cmp = _compress(chunks(wq + wkv), pe_v, w1_v, w2_v)
    sq = lambda a: a.reshape(bsz, seq, wq)
    skv = lambda a: a.reshape(bsz, seq, wkv)
    o_cmp, sel = _cmp_select(sq(q), k_cmp, v_cmp)
    o_sel = _sel_attention(sq(qr), skv(ks), skv(vs), sel)
    gates = gt[:, :NSA_HEADS * 3].reshape(bsz, seq, NSA_GROUPS, NSA_REP * 3)
    gates = jnp.pad(gates, ((0, 0), (0, 0), (0, 0), (0, LANES - NSA_REP * 3))).reshape(bsz, seq, NSA_GROUPS * LANES)
    o = _win_merge(sq(qr), skv(kw), skv(vw), o_cmp, o_sel, gates)
    return _proj_ln([o.reshape(-1, wq)], [w_out.astype(BF16)], h, ln_g, ln_b)


def kernel(x, a_w_in, a_lam_q1, a_lam_k1, a_lam_q2, a_lam_k2, a_subln_g, a_w_out, c_w_in, c_pe_k, c_w1_k, c_w2_k, c_pe_v, c_w1_v, c_w2_v, c_w_out, ln_mix_g, ln_mix_b, ln_ffn_g, ln_ffn_b, router_w, router_b, w_gu, w_down, ws_gu, ws_down):
    bsz, seq, d = x.shape
    h = x.reshape(bsz * seq, d)
    depth = ln_mix_g.shape[0]
    for i in range(depth):
        j = i // 2
        if i % 2 == 0:
            h = _layer_diff_moba(h, bsz, seq, a_w_in[j], a_lam_q1[j], a_lam_k1[j], a_lam_q2[j], a_lam_k2[j],
                                 a_subln_g[j], a_w_out[j], ln_mix_g[i], ln_mix_b[i], i)
        else:
            h = _layer_nsa(h, bsz, seq, c_w_in[j], c_pe_k[j], c_w1_k[j], c_w2_k[j], c_pe_v[j], c_w1_v[j],
                           c_w2_v[j], c_w_out[j], ln_mix_g[i], ln_mix_b[i])
        h = _moe_ln(h, router_w[i], router_b[i], w_gu[i], w_down[i], ws_gu[i], ws_down[i],
                    ln_ffn_g[i], ln_ffn_b[i])
    return h.reshape(bsz, seq, d)
```

```python
import functools
import math

import jax
import jax.numpy as jnp
import numpy as np
from jax import lax
from jax.experimental import pallas as pl
from jax.experimental.pallas import tpu as pltpu

F32 = jnp.float32
BF16 = jnp.bfloat16
I32 = jnp.int32

HEAD_DIM = 128
ROPE_THETA = 10000.0
NEG = -1e30
LN_EPS = 1e-5
DEPTH = 2
DN_ALPHA = (2 * DEPTH) ** 0.25

DIFF_HEADS = 8
MOBA_HEADS = 8
MOBA_BLOCK = 256
MOBA_TOPK = 3
NSA_HEADS = 16
NSA_GROUPS = 2
NSA_REP = NSA_HEADS // NSA_GROUPS
NSA_CMP_LEN = 32
NSA_CMP_STRIDE = 16
NSA_SEL_BLOCK = 64
NSA_SEL_N = 16
NSA_WINDOW = 512
NSA_FORCED = 1e4
N_EXPERTS = 64
EXPERT_DIM = 512
TOP_K = 8
N_GROUPS = 8
GROUP_SIZE = N_EXPERTS // N_GROUPS
TOPK_GROUPS = 4
ROUTED_SCALE = 2.5

LANES = 128
ISSUE_GROUP = 16
VMEM_LIMIT = 56 * 1024 * 1024

_NT = (((1,), (1,)), ((), ()))


def _cp(sem, **kw):
    return pltpu.CompilerParams(dimension_semantics=sem, vmem_limit_bytes=VMEM_LIMIT, **kw)


def _layernorm(x, g, b):
    mu = jnp.mean(x, axis=-1, keepdims=True)
    xc = x - mu
    var = jnp.mean(xc * xc, axis=-1, keepdims=True)
    return xc * lax.rsqrt(var + LN_EPS) * g + b


def _mm_kernel(x_ref, w_ref, o_ref):
    o_ref[...] = jnp.dot(x_ref[...].astype(BF16), w_ref[...],
                         preferred_element_type=F32).astype(o_ref.dtype)


def _matmul(x, w, tm, tn, out_dtype=F32):
    m, k = x.shape
    n = w.shape[1]
    tm, tn = min(tm, m), min(tn, n)
    return pl.pallas_call(
        _mm_kernel,
        grid=(m // tm, n // tn),
        in_specs=[pl.BlockSpec((tm, k), lambda i, j: (i, 0)),
                  pl.BlockSpec((k, tn), lambda i, j: (0, j))],
        out_specs=pl.BlockSpec((tm, tn), lambda i, j: (i, j)),
        out_shape=jax.ShapeDtypeStruct((m, n), out_dtype),
        compiler_params=_cp(("parallel", "arbitrary")),
        name="proj_in",
    )(x, w)


def _proj_ln_kernel(*refs, n_in, alpha):
    a_refs, w_refs = refs[:n_in], refs[n_in:2 * n_in]
    h_ref, g_ref, b_ref, o_ref = refs[2 * n_in:]
    acc = alpha * h_ref[...]
    for a_ref, w_ref in zip(a_refs, w_refs):
        acc = acc + jnp.dot(a_ref[...], w_ref[...], preferred_element_type=F32)
    o_ref[...] = _layernorm(acc, g_ref[...], b_ref[...])


def _proj_ln(acts, ws, h, g, b, tm=256):
    m, d = h.shape
    tm = min(tm, m)
    n_in = len(acts)
    in_specs = ([pl.BlockSpec((tm, a.shape[1]), lambda i: (i, 0)) for a in acts]
                + [pl.BlockSpec(w.shape, lambda i: (0, 0)) for w in ws]
                + [pl.BlockSpec((tm, d), lambda i: (i, 0)),
                   pl.BlockSpec((1, d), lambda i: (0, 0)),
                   pl.BlockSpec((1, d), lambda i: (0, 0))])
    return pl.pallas_call(
        functools.partial(_proj_ln_kernel, n_in=n_in, alpha=DN_ALPHA),
        grid=(m // tm,),
        in_specs=in_specs,
        out_specs=pl.BlockSpec((tm, d), lambda i: (i, 0)),
        out_shape=jax.ShapeDtypeStruct((m, d), F32),
        compiler_params=_cp(("parallel",)),
        name="proj_out_ln",
    )(*acts, *ws, h, g.reshape(1, d), b.reshape(1, d))


def _rope_tables(seq, d):
    half = d // 2
    inv = ROPE_THETA ** (-jnp.arange(half, dtype=F32) * 2.0 / d)
    ang = jnp.arange(seq, dtype=F32)[:, None] * inv[None, :]
    cos, sin = jnp.cos(ang), jnp.sin(ang)
    return jnp.concatenate([cos, cos], axis=1), jnp.concatenate([-sin, sin], axis=1)


def _rope128(x, cos, sin_signed):
    return x * cos + pltpu.roll(x, 64, 1) * sin_signed


def _rope64x2(x, cos, sin_lo, sin_hi):
    return x * cos + pltpu.roll(x, 96, 1) * sin_lo + pltpu.roll(x, 32, 1) * sin_hi


def _split0_kernel(p_ref, c64_ref, slo_ref, shi_ref, c128_ref, s128_ref,
                   dq_ref, dk_ref, dv_ref, mq_ref, mk_ref, mv_ref, km_ref, *, rows):
    c64, slo, shi = c64_ref[...], slo_ref[...], shi_ref[...]
    c128, s128 = c128_ref[...], s128_ref[...]
    w = DIFF_HEADS * HEAD_DIM
    dscale = (HEAD_DIM // 2) ** -0.5
    mscale = HEAD_DIM ** -0.5
    for h in range(DIFF_HEADS):
        lo, hi = h * HEAD_DIM, (h + 1) * HEAD_DIM
        dq_ref[:, lo:hi] = (_rope64x2(p_ref[:, lo:hi], c64, slo, shi) * dscale).astype(BF16)
        dk_ref[:, lo:hi] = _rope64x2(p_ref[:, w + lo:w + hi], c64, slo, shi).astype(BF16)
        dv_ref[:, lo:hi] = p_ref[:, 2 * w + lo:2 * w + hi].astype(BF16)
        mq_ref[:, lo:hi] = (_rope128(p_ref[:, 3 * w + lo:3 * w + hi], c128, s128) * mscale).astype(BF16)
        kr = _rope128(p_ref[:, 4 * w + lo:4 * w + hi], c128, s128)
        mk_ref[:, lo:hi] = kr.astype(BF16)
        km_ref[0, :, lo:hi] = jnp.sum(kr, axis=0, keepdims=True) * (1.0 / rows)
        mv_ref[:, lo:hi] = p_ref[:, 5 * w + lo:5 * w + hi].astype(BF16)


def _split0(p, seq):
    t = p.shape[0]
    rows = MOBA_BLOCK
    w = DIFF_HEADS * HEAD_DIM
    nb = seq // rows
    c64h, s64h = _rope_tables(seq, HEAD_DIM // 2)
    c64 = jnp.concatenate([c64h, c64h], axis=1)
    lane = jnp.arange(HEAD_DIM) % (HEAD_DIM // 2)
    s64 = jnp.concatenate([s64h, s64h], axis=1)
    slo = jnp.where(lane < HEAD_DIM // 4, s64, 0.0)
    shi = jnp.where(lane >= HEAD_DIM // 4, s64, 0.0)
    c128, s128 = _rope_tables(seq, HEAD_DIM)
    tab = pl.BlockSpec((rows, HEAD_DIM), lambda i: (i % nb, 0))
    out = pl.BlockSpec((rows, w), lambda i: (i, 0))
    bf = jax.ShapeDtypeStruct((t, w), BF16)
    return pl.pallas_call(
        functools.partial(_split0_kernel, rows=rows),
        grid=(t // rows,),
        in_specs=[pl.BlockSpec((rows, 6 * w), lambda i: (i, 0)), tab, tab, tab, tab, tab],
        out_specs=[out] * 6 + [pl.BlockSpec((1, 1, w), lambda i: (i, 0, 0))],
        out_shape=[bf] * 6 + [jax.ShapeDtypeStruct((t // rows, 1, w), F32)],
        compiler_params=_cp(("parallel",)),
        name="split_rope_l0",
    )(p, c64, slo, shi, c128, s128)


def _flash_update(carry, s, v):
    m, l, acc = carry
    m_new = jnp.maximum(m, jnp.max(s, axis=1, keepdims=True))
    p = jnp.exp(s - m_new)
    a = jnp.exp(m - m_new)
    l = a * l + jnp.sum(p, axis=1, keepdims=True)
    acc = a * acc + jnp.dot(p.astype(BF16), v, preferred_element_type=F32)
    return m_new, l, acc


def _flash_init(rows, d):
    return (jnp.full((rows, 1), NEG, F32), jnp.zeros((rows, 1), F32), jnp.zeros((rows, d), F32))


def _diff_kernel(lam_ref, q_ref, k_ref, v_ref, g_ref, o_ref, *, tq, out_scale):
    i = pl.program_id(2)
    q = q_ref[0]
    lane = lax.broadcasted_iota(I32, (tq, HEAD_DIM), 1)
    zero = jnp.zeros_like(q)
    qs = jnp.concatenate([jnp.where(lane < HEAD_DIM // 2, q, zero),
                          jnp.where(lane >= HEAD_DIM // 2, q, zero)], axis=0)

    def tile(j):
        st = pl.multiple_of(j * tq, tq)
        k = k_ref[0, pl.ds(st, tq), :]
        v = v_ref[0, pl.ds(st, tq), :]
        return lax.dot_general(qs, k, _NT, preferred_element_type=F32), v

    def body(j, carry):
        s, v = tile(j)
        return _flash_update(carry, s, v)

    carry = lax.fori_loop(0, i, body, _flash_init(2 * tq, HEAD_DIM))
    s, v = tile(i)
    r = lax.broadcasted_iota(I32, (2 * tq, tq), 0)
    c = lax.broadcasted_iota(I32, (2 * tq, tq), 1)
    s = jnp.where(c <= jnp.where(r >= tq, r - tq, r), s, NEG)
    _, l, acc = _flash_update(carry, s, v)
    o = acc / l
    od = o[:tq] - lam_ref[0] * o[tq:]
    od = od * lax.rsqrt(jnp.mean(od * od, axis=1, keepdims=True) + LN_EPS) * g_ref[...] * out_scale
    o_ref[0] = od.astype(o_ref.dtype)


def _diff_attention(lam, q, k, v, subln_g, lam_init, tq=1024):
    b, s, _ = q.shape
    tq = min(tq, s)
    kv = pl.BlockSpec((1, s, HEAD_DIM), lambda bi, h, i, *_: (bi, 0, h))
    qo = pl.BlockSpec((1, tq, HEAD_DIM), lambda bi, h, i, *_: (bi, i, h))
    return pl.pallas_call(
        functools.partial(_diff_kernel, tq=tq, out_scale=1.0 - lam_init),
        grid_spec=pltpu.PrefetchScalarGridSpec(
            num_scalar_prefetch=1,
            grid=(b, DIFF_HEADS, s // tq),
            in_specs=[qo, kv, kv, pl.BlockSpec((1, HEAD_DIM), lambda bi, h, i, *_: (0, 0))],
            out_specs=qo),
        out_shape=jax.ShapeDtypeStruct(q.shape, BF16),
        compiler_params=_cp(("parallel", "parallel", "arbitrary")),
        name="diff_attention",
    )(lam, q, k, v, subln_g.reshape(1, HEAD_DIM))


def _moba_kernel(q_ref, k_ref, v_ref, km_ref, o_ref, *, tq, blk, nb, n_pick, hp):
    i = pl.program_id(2)
    bid = lax.broadcasted_iota(I32, (tq, nb), 1)
    own = (i * tq + lax.broadcasted_iota(I32, (tq, nb), 0)) // blk
    past = bid < own
    own_col = (i * tq + lax.broadcasted_iota(I32, (tq, 1), 0)) // blk
    rpos = lax.broadcasted_iota(I32, (tq, blk), 0) & (blk - 1)
    kcol = lax.broadcasted_iota(I32, (tq, blk), 1)
    causal = kcol <= rpos
    heads = [slice(h * HEAD_DIM, (h + 1) * HEAD_DIM) for h in range(hp)]

    qs, sels = [], []
    for hs in heads:
        q = q_ref[0, :, hs]
        gate = lax.dot_general(q.astype(F32), km_ref[0, :, hs], _NT, precision=lax.Precision.HIGHEST,
                               preferred_element_type=F32)
        g = jnp.where(past, gate, NEG)
        sel = jnp.zeros((tq, nb), F32)
        for _ in range(n_pick):
            mx = jnp.max(g, axis=1, keepdims=True)
            idx = jnp.min(jnp.where(g == mx, bid, nb), axis=1, keepdims=True)
            pick = bid == idx
            sel = jnp.where(pick, jnp.where(past, 1.0, 0.0), sel)
            g = jnp.where(pick, -jnp.inf, g)
        qs.append(q)
        sels.append(sel)

    def body(j, carries):
        st = pl.multiple_of(j * blk, blk)
        out = []
        for hs, q, sel, carry in zip(heads, qs, sels, carries):
            s = lax.dot_general(q, k_ref[0, pl.ds(st, blk), hs], _NT, preferred_element_type=F32)
            picked = jnp.sum(jnp.where(bid == j, sel, 0.0), axis=1, keepdims=True) > 0.5
            keep = picked | ((own_col == j) & causal)
            out.append(_flash_update(carry, jnp.where(keep, s, NEG), v_ref[0, pl.ds(st, blk), hs]))
        return tuple(out)

    n_blocks = (i + 1) * (tq // blk)
    carries = lax.fori_loop(0, n_blocks, body, tuple(_flash_init(tq, HEAD_DIM) for _ in heads))
    for hs, (_, l, acc) in zip(heads, carries):
        o_ref[0, :, hs] = (acc / l).astype(o_ref.dtype)


def _moba_attention(q, k, v, kmean, tq=1024, hp=2):
    b, s, _ = q.shape
    blk = min(MOBA_BLOCK, s)
    tq = min(tq, s)
    nb = s // blk
    n_pick = max(1, min(MOBA_TOPK, nb - 1))
    kv = pl.BlockSpec((1, s, hp * HEAD_DIM), lambda bi, h, i: (bi, 0, h))
    qo = pl.BlockSpec((1, tq, hp * HEAD_DIM), lambda bi, h, i: (bi, i, h))
    return pl.pallas_call(
        functools.partial(_moba_kernel, tq=tq, blk=blk, nb=nb, n_pick=n_pick, hp=hp),
        grid=(b, MOBA_HEADS // hp, s // tq),
        in_specs=[qo, kv, kv, pl.BlockSpec((1, nb, hp * HEAD_DIM), lambda bi, h, i: (bi, 0, h))],
        out_specs=qo,
        out_shape=jax.ShapeDtypeStruct(q.shape, BF16),
        compiler_params=_cp(("parallel", "parallel", "arbitrary")),
        name="moba_attention",
    )(q, k, v, kmean)


def _split1_kernel(p_ref, c_ref, s_ref, q_ref, qr_ref, ks_ref, vs_ref, kw_ref, vw_ref, gt_ref):
    cos, sin = c_ref[...], s_ref[...]
    scale = HEAD_DIM ** -0.5
    wq = NSA_HEADS * HEAD_DIM
    wkv = NSA_GROUPS * HEAD_DIM
    for h in range(NSA_HEADS):
        lo, hi = h * HEAD_DIM, (h + 1) * HEAD_DIM
        x = p_ref[:, lo:hi]
        q_ref[:, lo:hi] = (x * scale).astype(BF16)
        qr_ref[:, lo:hi] = (_rope128(x, cos, sin) * scale).astype(BF16)
    base = wq + 2 * wkv
    for g in range(NSA_GROUPS):
        lo, hi = g * HEAD_DIM, (g + 1) * HEAD_DIM
        ks_ref[:, lo:hi] = _rope128(p_ref[:, base + lo:base + hi], cos, sin).astype(BF16)
        vs_ref[:, lo:hi] = p_ref[:, base + wkv + lo:base + wkv + hi].astype(BF16)
        kw_ref[:, lo:hi] = _rope128(p_ref[:, base + 2 * wkv + lo:base + 2 * wkv + hi], cos, sin).astype(BF16)
        vw_ref[:, lo:hi] = p_ref[:, base + 3 * wkv + lo:base + 3 * wkv + hi].astype(BF16)
    gt_ref[...] = jax.nn.sigmoid(p_ref[:, base + 4 * wkv:base + 4 * wkv + LANES])


def _split1(p, seq, rows=256):
    t, wp = p.shape
    rows = min(rows, seq)
    nb = seq // rows
    wq = NSA_HEADS * HEAD_DIM
    wkv = NSA_GROUPS * HEAD_DIM
    cos, sin = _rope_tables(seq, HEAD_DIM)
    tab = pl.BlockSpec((rows, HEAD_DIM), lambda i: (i % nb, 0))
    oq = pl.BlockSpec((rows, wq), lambda i: (i, 0))
    okv = pl.BlockSpec((rows, wkv), lambda i: (i, 0))
    sq = jax.ShapeDtypeStruct((t, wq), BF16)
    skv = jax.ShapeDtypeStruct((t, wkv), BF16)
    return pl.pallas_call(
        _split1_kernel,
        grid=(t // rows,),
        in_specs=[pl.BlockSpec((rows, wp), lambda i: (i, 0)), tab, tab],
        out_specs=[oq, oq, okv, okv, okv, okv, pl.BlockSpec((rows, LANES), lambda i: (i, 0))],
        out_shape=[sq, sq, skv, skv, skv, skv, jax.ShapeDtypeStruct((t, LANES), F32)],
        compiler_params=_cp(("parallel",)),
        name="split_rope_l1",
    )(p, cos, sin)


def _gelu_tanh(x):
    return 0.5 * x * (1.0 + jnp.tanh(math.sqrt(2.0 / math.pi) * (x + 0.044715 * (x * x * x))))


def _compress_kernel(c_ref, pelo_ref, pehi_ref, w1lo_ref, w1hi_ref, w2_ref, o_ref, *, nc):
    c = c_ref[0]
    a = jnp.dot((c + pelo_ref[...]).astype(BF16), w1lo_ref[...], preferred_element_type=F32)
    bm = jnp.dot((c + pehi_ref[...]).astype(BF16), w1hi_ref[...], preferred_element_type=F32)
    hid = _gelu_tanh(a + pltpu.roll(bm, nc - 1, 0))
    o_ref[0] = jnp.dot(hid.astype(BF16), w2_ref[...], preferred_element_type=F32).astype(o_ref.dtype)


def _compress(chunks, pe, w1, w2):
    bg, nc, cw = chunks.shape
    half = NSA_CMP_STRIDE
    pelo = pe[:half].reshape(1, cw)
    pehi = pe[half:].reshape(1, cw)
    w1lo = w1[:cw].astype(BF16)
    w1hi = w1[cw:].astype(BF16)
    full = lambda a: pl.BlockSpec(a.shape, lambda i: (0,) * a.ndim)
    return pl.pallas_call(
        functools.partial(_compress_kernel, nc=nc),
        grid=(bg,),
        in_specs=[pl.BlockSpec((1, nc, cw), lambda i: (i, 0, 0)), full(pelo), full(pehi),
                  full(w1lo), full(w1hi), pl.BlockSpec(w2.shape, lambda i: (0, 0))],
        out_specs=pl.BlockSpec((1, nc, HEAD_DIM), lambda i: (i, 0, 0)),
        out_shape=jax.ShapeDtypeStruct((bg, nc, HEAD_DIM), BF16),
        compiler_params=_cp(("parallel",)),
        name="nsa_compress",
    )(chunks, pelo, pehi, w1lo, w1hi, w2.astype(BF16))


def _stack_heads(q):
    return jnp.concatenate([q[:, r * HEAD_DIM:(r + 1) * HEAD_DIM] for r in range(NSA_REP)], axis=0)


def _unstack_store(o_ref, o, tq):
    for r in range(NSA_REP):
        o_ref[0, :, r * HEAD_DIM:(r + 1) * HEAD_DIM] = o[r * tq:(r + 1) * tq].astype(o_ref.dtype)


def _cmp_select_kernel(q_ref, kc_ref, vc_ref, o_ref, sel_ref, *, tq, nc, nblk, n_pick):
    i = pl.program_id(2)
    qs = _stack_heads(q_ref[0])
    s = lax.dot_general(qs, kc_ref[0], _NT, preferred_element_type=F32)
    rows = NSA_REP * tq
    row = lax.broadcasted_iota(I32, (rows, nc), 0)
    n = lax.broadcasted_iota(I32, (rows, nc), 1)
    pos = i * tq + (row & (tq - 1))
    valid = n * NSA_CMP_STRIDE + (NSA_CMP_LEN - 1) <= pos
    sm = jnp.where(valid, s, NEG)
    e = jnp.exp(sm - jnp.max(sm, axis=1, keepdims=True))
    p = jnp.where(valid, e / jnp.sum(e, axis=1, keepdims=True), 0.0)
    o = jnp.dot(p.astype(BF16), vc_ref[0], preferred_element_type=F32)
    _unstack_store(o_ref, o, tq)

    psum = p[0:tq]
    for r in range(1, NSA_REP):
        psum = psum + p[r * tq:(r + 1) * tq]
    cn = lax.broadcasted_iota(I32, (nc, nblk), 0) * NSA_CMP_STRIDE
    bj = lax.broadcasted_iota(I32, (nc, nblk), 1) * NSA_SEL_BLOCK
    ov = jnp.maximum(jnp.minimum(cn + NSA_CMP_LEN, bj + NSA_SEL_BLOCK) - jnp.maximum(cn, bj), 0)
    ov = ov.astype(F32) * (1.0 / NSA_CMP_LEN)
    imp = jnp.dot(psum, ov, precision=lax.Precision.HIGHEST, preferred_element_type=F32)

    cur = (i * tq + lax.broadcasted_iota(I32, (tq, nblk), 0)) // NSA_SEL_BLOCK
    blk = lax.broadcasted_iota(I32, (tq, nblk), 1)
    allowed = blk <= cur
    forced = (blk == 0) | (blk == cur) | (blk == cur - 1)
    val = jnp.where(allowed, jnp.where(forced, NSA_FORCED, imp), NEG)
    sel = jnp.zeros((tq, nblk), F32)
    for _ in range(n_pick):
        mx = jnp.max(val, axis=1, keepdims=True)
        idx = jnp.min(jnp.where(val == mx, blk, nblk), axis=1, keepdims=True)
        pick = blk == idx
        sel = jnp.where(pick, 1.0, sel)
        val = jnp.where(pick, -jnp.inf, val)
    sel_ref[0, 0] = jnp.where(allowed, sel, 0.0)


def _cmp_select(q, kcmp, vcmp, tq=256):
    b, s, _ = q.shape
    tq = min(tq, s)
    nc = kcmp.shape[1]
    nblk = s // NSA_SEL_BLOCK
    n_pick = min(NSA_SEL_N, nblk)
    wg = NSA_REP * HEAD_DIM
    qo = pl.BlockSpec((1, tq, wg), lambda bi, g, i: (bi, i, g))
    kv = pl.BlockSpec((1, nc, HEAD_DIM), lambda bi, g, i: (bi * NSA_GROUPS + g, 0, 0))
    return pl.pallas_call(
        functools.partial(_cmp_select_kernel, tq=tq, nc=nc, nblk=nblk, n_pick=n_pick),
        grid=(b, NSA_GROUPS, s // tq),
        in_specs=[qo, kv, kv],
        out_specs=[qo, pl.BlockSpec((1, 1, tq, nblk), lambda bi, g, i: (bi, g, i, 0))],
        out_shape=[jax.ShapeDtypeStruct(q.shape, F32),
                   jax.ShapeDtypeStruct((b, NSA_GROUPS, s, nblk), F32)],
        compiler_params=_cp(("parallel", "parallel", "arbitrary")),
        name="nsa_cmp_select",
    )(q, kcmp, vcmp)


def _mask_heads(s, keep, tq):
    tk = s.shape[1]
    s3 = jnp.where(keep[None], s.reshape(NSA_REP, tq, tk), NEG)
    return s3.reshape(NSA_REP * tq, tk)


def _sel_kernel(q_ref, k_ref, v_ref, sel_ref, o_ref, *, tq, tk, nblk):
    i = pl.program_id(2)
    qs = _stack_heads(q_ref[0])
    sel = sel_ref[0, 0].astype(BF16)
    per = tk // NSA_SEL_BLOCK
    qpos = i * tq + lax.broadcasted_iota(I32, (tq, tk), 0)
    kcol = lax.broadcasted_iota(I32, (tq, tk), 1)
    eb = lax.broadcasted_iota(I32, (nblk, tk), 0)
    ec = lax.broadcasted_iota(I32, (nblk, tk), 1) // NSA_SEL_BLOCK

    def body(j, carry):
        st = pl.multiple_of(j * tk, tk)
        k = k_ref[0, pl.ds(st, tk), :]
        v = v_ref[0, pl.ds(st, tk), :]
        s = lax.dot_general(qs, k, _NT, preferred_element_type=F32)
        expand = jnp.where(eb == j * per + ec, 1.0, 0.0).astype(BF16)
        on = jnp.dot(sel, expand, preferred_element_type=F32)
        keep = jnp.where(j * tk + kcol <= qpos, on, 0.0) > 0.5
        return _flash_update(carry, _mask_heads(s, keep, tq), v)

    n_tiles = (i * tq + tq - 1) // tk + 1
    _, l, acc = lax.fori_loop(0, n_tiles, body, _flash_init(NSA_REP * tq, HEAD_DIM))
    _unstack_store(o_ref, acc / l, tq)


def _sel_attention(q, k, v, sel, tq=128, tk=512):
    b, s, _ = q.shape
    tq, tk = min(tq, s), min(tk, s)
    nblk = s // NSA_SEL_BLOCK
    wg = NSA_REP * HEAD_DIM
    qo = pl.BlockSpec((1, tq, wg), lambda bi, g, i: (bi, i, g))
    kv = pl.BlockSpec((1, s, HEAD_DIM), lambda bi, g, i: (bi, 0, g))
    return pl.pallas_call(
        functools.partial(_sel_kernel, tq=tq, tk=tk, nblk=nblk),
        grid=(b, NSA_GROUPS, s // tq),
        in_specs=[qo, kv, kv, pl.BlockSpec((1, 1, tq, nblk), lambda bi, g, i: (bi, g, i, 0))],
        out_specs=qo,
        out_shape=jax.ShapeDtypeStruct(q.shape, F32),
        compiler_params=_cp(("parallel", "parallel", "arbitrary")),
        name="nsa_sel_attention",
    )(q, k, v, sel)


def _win_kernel(q_ref, k_ref, v_ref, oc_ref, os_ref, gt_ref, o_ref, *, tq, window):
    i = pl.program_id(2)
    qs = _stack_heads(q_ref[0])
    qpos = i * tq + lax.broadcasted_iota(I32, (tq, tq), 0)
    kcol = lax.broadcasted_iota(I32, (tq, tq), 1)

    def body(j, carry):
        st = pl.multiple_of(j * tq, tq)
        k = k_ref[0, pl.ds(st, tq), :]
        v = v_ref[0, pl.ds(st, tq), :]
        s = lax.dot_general(qs, k, _NT, preferred_element_type=F32)
        kpos = j * tq + kcol
        keep = (kpos <= qpos) & (kpos > qpos - window)
        return _flash_update(carry, _mask_heads(s, keep, tq), v)

    def rev_body(t, carry):
        return body(i - t, carry)

    n_tiles = jnp.minimum(i, window // tq) + 1
    _, l, acc = lax.fori_loop(0, n_tiles, rev_body, _flash_init(NSA_REP * tq, HEAD_DIM))
    ow = acc / l
    gt = gt_ref[0]
    for r in range(NSA_REP):
        lo, hi = r * HEAD_DIM, (r + 1) * HEAD_DIM
        mix = (gt[:, 3 * r:3 * r + 1] * oc_ref[0, :, lo:hi]
               + gt[:, 3 * r + 1:3 * r + 2] * os_ref[0, :, lo:hi]
               + gt[:, 3 * r + 2:3 * r + 3] * ow[r * tq:(r + 1) * tq])
        o_ref[0, :, lo:hi] = mix.astype(o_ref.dtype)


def _win_merge(q, k, v, o_cmp, o_sel, gates, tq=256):
    b, s, _ = q.shape
    tq = min(tq, s)
    wg = NSA_REP * HEAD_DIM
    qo = pl.BlockSpec((1, tq, wg), lambda bi, g, i: (bi, i, g))
    kv = pl.BlockSpec((1, s, HEAD_DIM), lambda bi, g, i: (bi, 0, g))
    return pl.pallas_call(
        functools.partial(_win_kernel, tq=tq, window=NSA_WINDOW),
        grid=(b, NSA_GROUPS, s // tq),
        in_specs=[qo, kv, kv, qo, qo, pl.BlockSpec((1, tq, LANES), lambda bi, g, i: (bi, i, g))],
        out_specs=qo,
        out_shape=jax.ShapeDtypeStruct(q.shape, BF16),
        compiler_params=_cp(("parallel", "parallel", "arbitrary")),
        name="nsa_window_merge",
    )(q, k, v, o_cmp, o_sel, gates)


def _router_kernel(x_ref, w_ref, b_ref, idx_ref, wt_ref):
    tm = x_ref.shape[0]
    logits = jnp.dot(x_ref[...], w_ref[...], precision=lax.Precision.HIGHEST,
                     preferred_element_type=F32)
    score = jax.nn.sigmoid(logits.T[:N_EXPERTS])
    biased = score + b_ref[...]
    eid = lax.broadcasted_iota(I32, (N_EXPERTS, tm), 0)
    grp = eid // GROUP_SIZE

    b3 = biased.reshape(N_GROUPS, GROUP_SIZE, tm)
    e3 = eid.reshape(N_GROUPS, GROUP_SIZE, tm)
    m1 = jnp.max(b3, axis=1, keepdims=True)
    i1 = jnp.min(jnp.where(b3 == m1, e3, N_EXPERTS), axis=1, keepdims=True)
    m2 = jnp.max(jnp.where(e3 == i1, -jnp.inf, b3), axis=1, keepdims=True)
    gscore = jnp.broadcast_to(m1 + m2, (N_GROUPS, GROUP_SIZE, tm)).reshape(N_EXPERTS, tm)

    emask = jnp.zeros((N_EXPERTS, tm), F32)
    for _ in range(TOPK_GROUPS):
        mx = jnp.max(gscore, axis=0, keepdims=True)
        gi = jnp.min(jnp.where(gscore == mx, grp, N_GROUPS), axis=0, keepdims=True)
        pick = grp == gi
        emask = jnp.where(pick, 1.0, emask)
        gscore = jnp.where(pick, -jnp.inf, gscore)

    val = jnp.where(emask > 0.5, biased, NEG)
    krow = lax.broadcasted_iota(I32, (TOP_K, tm), 0)
    idx_out = jnp.zeros((TOP_K, tm), I32)
    wt_out = jnp.zeros((TOP_K, tm), F32)
    for kk in range(TOP_K):
        mx = jnp.max(val, axis=0, keepdims=True)
        ei = jnp.min(jnp.where(val == mx, eid, N_EXPERTS), axis=0, keepdims=True)
        pick = eid == ei
        wk = jnp.sum(jnp.where(pick, score, 0.0), axis=0, keepdims=True)
        idx_out = jnp.where(krow == kk, ei, idx_out)
        wt_out = jnp.where(krow == kk, wk, wt_out)
        val = jnp.where(pick, -jnp.inf, val)
    idx_ref[...] = idx_out
    wt_ref[...] = wt_out / jnp.sum(wt_out, axis=0, keepdims=True) * ROUTED_SCALE


def _router(x, w, bias, tm=512):
    t, d = x.shape
    tm = min(tm, t)
    idx, wts = pl.pallas_call(
        _router_kernel,
        grid=(t // tm,),
        in_specs=[pl.BlockSpec((tm, d), lambda i: (i, 0)),
                  pl.BlockSpec((d, LANES), lambda i: (0, 0)),
                  pl.BlockSpec((N_EXPERTS, 1), lambda i: (0, 0))],
        out_specs=[pl.BlockSpec((TOP_K, tm), lambda i: (0, i)),
                   pl.BlockSpec((TOP_K, tm), lambda i: (0, i))],
        out_shape=[jax.ShapeDtypeStruct((TOP_K, t), I32), jax.ShapeDtypeStruct((TOP_K, t), F32)],
        compiler_params=_cp(("parallel",)),
        name="moe_router",
    )(x, jnp.pad(w, ((0, 0), (0, LANES - N_EXPERTS))), bias.reshape(N_EXPERTS, 1))
    return idx.T, wts.T


def _moe_plan(eidx, tm):
    t = eidx.shape[0]
    n_slots = t * TOP_K
    flat_e = eidx.reshape(-1)
    order = jnp.argsort(flat_e, stable=True).astype(I32)
    rank = jnp.argsort(order).astype(I32)
    experts = jnp.arange(N_EXPERTS, dtype=I32)
    counts = jnp.sum((flat_e[:, None] == experts[None, :]).astype(I32), axis=0)
    pcounts = (counts + tm - 1) // tm * tm
    pend = jnp.cumsum(pcounts)
    pstart = pend - pcounts
    cstart = jnp.cumsum(counts) - counts
    n_rows = n_slots + N_EXPERTS * tm
    n_tiles = n_rows // tm
    tile_start = jnp.arange(n_tiles, dtype=I32) * tm
    tile_e = jnp.clip(jnp.sum((pend[None, :] <= tile_start[:, None]).astype(I32), axis=1), 0, N_EXPERTS - 1)
    j = tile_start[:, None] + jnp.arange(tm, dtype=I32)[None, :] - pstart[tile_e][:, None]
    src = jnp.clip(cstart[tile_e][:, None] + j, 0, n_slots - 1)
    row_tok = jnp.where(j < counts[tile_e][:, None], order[src] // TOP_K, 0).astype(I32)
    slot_row = ((pstart - cstart)[flat_e] + rank).astype(I32)
    n_used = (pend[-1] // tm).astype(I32).reshape(1)
    return row_tok, slot_row, tile_e.astype(I32), n_used


def _gather_rows_pipelined(i, n_steps, idx_hbm, src_hbm, idx_smem, buf, idx_sem, row_sem, rows):
    slot = i % 2

    def idx_copy(t, s):
        return pltpu.make_async_copy(idx_hbm.at[t], idx_smem[s], idx_sem.at[s])

    def issue_rows(s):
        def issue(g, c):
            for k in range(ISSUE_GROUP):
                pltpu.make_async_copy(src_hbm.at[pl.ds(idx_smem[s][g * ISSUE_GROUP + k], 1)],
                                      buf.at[s, g * (ISSUE_GROUP // 8) + k // 8, pl.ds(k % 8, 1)],
                                      row_sem.at[s]).start()
            return c
        lax.fori_loop(0, rows // ISSUE_GROUP, issue, 0)

    @pl.when(i == 0)
    def _():
        idx_copy(0, 0).start()
        idx_copy(0, 0).wait()
        issue_rows(0)

        @pl.when(n_steps > 1)
        def _():
            idx_copy(1, 1).start()

    for s in range(2):
        @pl.when(jnp.logical_and(i + 1 < n_steps, slot == 1 - s))
        def _():
            idx_copy(i + 1, s).wait()
            issue_rows(s)

            @pl.when(i + 2 < n_steps)
            def _():
                idx_copy(i + 2, 1 - s).start()

    @pl.when(i < n_steps)
    def _():
        pltpu.make_async_copy(buf.at[slot], buf.at[slot], row_sem.at[slot]).wait()


def _expert_kernel(te_ref, nused_ref, tok_hbm, x_hbm, wgu_ref, wd_ref, o_ref,
                   idx0, idx1, xbuf, idx_sem, row_sem, *, tm):
    i = pl.program_id(0)
    n_used = nused_ref[0]
    d = x_hbm.shape[1]
    _gather_rows_pipelined(i, n_used, tok_hbm, x_hbm, (idx0, idx1), xbuf, idx_sem, row_sem, tm)

    @pl.when(i < n_used)
    def _():
        gu = jnp.dot(xbuf[i % 2].reshape(tm, d).astype(BF16), wgu_ref[0, 0].astype(BF16),
                     preferred_element_type=F32)
        act = jax.nn.silu(gu[:, :EXPERT_DIM]) * gu[:, EXPERT_DIM:]
        o_ref[...] = jnp.dot(act.astype(BF16), wd_ref[0, 0].astype(BF16), preferred_element_type=F32)

    @pl.when(i >= n_used)
    def _():
        o_ref[...] = jnp.zeros_like(o_ref)


def _expert_ffn(x, row_tok, w_gu, w_down, layer, tile_e, n_used):
    n_tiles, tm = row_tok.shape
    d = x.shape[1]
    last = lambda i, te, nu: jnp.minimum(i, nu[0] - 1)
    return pl.pallas_call(
        functools.partial(_expert_kernel, tm=tm),
        grid_spec=pltpu.PrefetchScalarGridSpec(
            num_scalar_prefetch=2,
            grid=(n_tiles,),
            in_specs=[pl.BlockSpec(memory_space=pl.ANY), pl.BlockSpec(memory_space=pl.ANY),
                      pl.BlockSpec((1, 1, d, 2 * EXPERT_DIM),
                                   lambda i, te, nu: (layer, te[last(i, te, nu)], 0, 0)),
                      pl.BlockSpec((1, 1, EXPERT_DIM, d),
                                   lambda i, te, nu: (layer, te[last(i, te, nu)], 0, 0))],
            out_specs=pl.BlockSpec((tm, d), lambda i, te, nu: (i, 0)),
            scratch_shapes=[pltpu.SMEM((tm,), I32), pltpu.SMEM((tm,), I32), pltpu.VMEM((2, tm // 8, 8, d), F32),
                            pltpu.SemaphoreType.DMA((2,)), pltpu.SemaphoreType.DMA((2,))]),
        out_shape=jax.ShapeDtypeStruct((n_tiles * tm, d), F32),
        compiler_params=_cp(("arbitrary",)),
        name="moe_expert_ffn",
    )(tile_e, n_used, row_tok, x, w_gu, w_down)


def _shared_kernel(x_ref, wgu_ref, wd_ref, o_ref):
    gu = jnp.dot(x_ref[...].astype(BF16), wgu_ref[...], preferred_element_type=F32)
    act = jax.nn.silu(gu[:, :EXPERT_DIM]) * gu[:, EXPERT_DIM:]
    o_ref[...] = jnp.dot(act.astype(BF16), wd_ref[...], preferred_element_type=F32)


def _shared_ffn(x, w_gu, w_down, tm=512):
    t, d = x.shape
    tm = min(tm, t)
    return pl.pallas_call(
        _shared_kernel,
        grid=(t // tm,),
        in_specs=[pl.BlockSpec((tm, d), lambda i: (i, 0)),
                  pl.BlockSpec(w_gu.shape, lambda i: (0, 0)),
                  pl.BlockSpec(w_down.shape, lambda i: (0, 0))],
        out_specs=pl.BlockSpec((tm, d), lambda i: (i, 0)),
        out_shape=jax.ShapeDtypeStruct((t, d), F32),
        compiler_params=_cp(("parallel",)),
        name="moe_shared_ffn",
    )(x, w_gu, w_down)


def _combine_kernel(rows_hbm, ys_hbm, wt_ref, h_ref, sh_ref, g_ref, b_ref, o_ref,
                    idx0, idx1, buf, idx_sem, row_sem, *, tt, steps):
    i = pl.program_id(0)
    d = h_ref.shape[1]
    _gather_rows_pipelined(i, steps, rows_hbm, ys_hbm, (idx0, idx1), buf, idx_sem, row_sem, tt * TOP_K)
    wt = wt_ref[...]
    acc = DN_ALPHA * h_ref[...] + sh_ref[...]
    for k in range(TOP_K):
        yk = buf[i % 2, k * (tt // 8):(k + 1) * (tt // 8)].reshape(tt, d)
        acc = acc + wt[:, k:k + 1] * yk
    o_ref[...] = _layernorm(acc, g_ref[...], b_ref[...])


def _combine_ln(ys, slot_row, wts, h, shared, g, b, tt=64):
    t, d = h.shape
    tt = min(tt, t)
    steps = t // tt
    blk = pl.BlockSpec((tt, d), lambda i: (i, 0))
    vec = pl.BlockSpec((1, d), lambda i: (0, 0))
    return pl.pallas_call(
        functools.partial(_combine_kernel, tt=tt, steps=steps),
        grid=(steps,),
        in_specs=[pl.BlockSpec(memory_space=pl.ANY), pl.BlockSpec(memory_space=pl.ANY),
                  pl.BlockSpec((tt, TOP_K), lambda i: (i, 0)), blk, blk, vec, vec],
        out_specs=blk,
        out_shape=jax.ShapeDtypeStruct((t, d), F32),
        scratch_shapes=[pltpu.SMEM((tt * TOP_K,), I32), pltpu.SMEM((tt * TOP_K,), I32),
                        pltpu.VMEM((2, tt * TOP_K // 8, 8, d), F32),
                        pltpu.SemaphoreType.DMA((2,)), pltpu.SemaphoreType.DMA((2,))],
        compiler_params=_cp(("arbitrary",)),
        name="moe_combine_ln",
    )(slot_row.reshape(steps, tt, TOP_K).transpose(0, 2, 1).reshape(steps, tt * TOP_K),
      ys, wts, h, shared, g.reshape(1, d), b.reshape(1, d))


def _moe_ln(h, layer, router_w, router_b, w_gu, w_down, ws_gu, ws_down, g, b, tm=512):
    eidx, wts = _router(h, router_w, router_b)
    row_tok, slot_row, tile_e, n_used = _moe_plan(eidx, tm)
    ys = _expert_ffn(h, row_tok, w_gu, w_down, layer, tile_e, n_used)
    shared = _shared_ffn(h, ws_gu.astype(BF16), ws_down.astype(BF16))
    return _combine_ln(ys, slot_row, wts, h, shared, g, b)


def _layer_diff_moba(h, bsz, seq, w_in, lam_q1, lam_k1, lam_q2, lam_k2, subln_g, w_out, ln_g, ln_b, layer_idx):
    w = DIFF_HEADS * HEAD_DIM
    p = _matmul(h, w_in.astype(BF16), 1024, 768)
    dq, dk, dv, mq, mk, mv, kmean = _split0(p, seq)
    lam_init = 0.8 - 0.6 * math.exp(-0.3 * layer_idx)
    lam = (jnp.exp(jnp.sum(lam_q1 * lam_k1)) - jnp.exp(jnp.sum(lam_q2 * lam_k2)) + lam_init).reshape(1)
    sh = lambda a: a.reshape(bsz, seq, w)
    o_a = _diff_attention(lam, sh(dq), sh(dk), sh(dv), subln_g, lam_init)
    o_b = _moba_attention(sh(mq), sh(mk), sh(mv), kmean.reshape(bsz, seq // MOBA_BLOCK, w))
    w_out = w_out.astype(BF16)
    return _proj_ln([o_a.reshape(-1, w), o_b.reshape(-1, w)], [w_out[:w], w_out[w:]], h, ln_g, ln_b)


def _layer_nsa(h, bsz, seq, w_in, pe_k, w1_k, w2_k, pe_v, w1_v, w2_v, w_out, ln_g, ln_b):
    wq = NSA_HEADS * HEAD_DIM
    wkv = NSA_GROUPS * HEAD_DIM
    n_in = w_in.shape[1]
    n_pad = -(-(n_in + LANES) // 768) * 768
    p = _matmul(h, jnp.pad(w_in, ((0, 0), (0, n_pad - n_in))).astype(BF16), 1024, 768)
    q, qr, ks, vs, kw, vw, gt = _split1(p, seq)

    def chunks(col):
        a = p[:, col:col + wkv].reshape(bsz, seq, NSA_GROUPS, HEAD_DIM).transpose(0, 2, 1, 3)
        return a.reshape(bsz * NSA_GROUPS, seq // NSA_CMP_STRIDE, NSA_CMP_STRIDE * HEAD_DIM)

    k_cmp = _compress(chunks(wq), pe_k, w1_k, w2_k)
    v_cmp = _compress(chunks(wq + wkv), pe_v, w1_v, w2_v)
    sq = lambda a: a.reshape(bsz, seq, wq)
    skv = lambda a: a.reshape(bsz, seq, wkv)
    o_cmp, sel = _cmp_select(sq(q), k_cmp, v_cmp)
    o_sel = _sel_attention(sq(qr), skv(ks), skv(vs), sel)
    gates = gt[:, :NSA_HEADS * 3].reshape(bsz, seq, NSA_GROUPS, NSA_REP * 3)
    gates = jnp.pad(gates, ((0, 0), (0, 0), (0, 0), (0, LANES - NSA_REP * 3))).reshape(bsz, seq, NSA_GROUPS * LANES)
    o = _win_merge(sq(qr), skv(kw), skv(vw), o_cmp, o_sel, gates)
    return _proj_ln([o.reshape(-1, wq)], [w_out.astype(BF16)], h, ln_g, ln_b)


def kernel(x, a_w_in, a_lam_q1, a_lam_k1, a_lam_q2, a_lam_k2, a_subln_g, a_w_out, c_w_in, c_pe_k, c_w1_k, c_w2_k, c_pe_v, c_w1_v, c_w2_v, c_w_out, ln_mix_g, ln_mix_b, ln_ffn_g, ln_ffn_b, router_w, router_b, w_gu, w_down, ws_gu, ws_down):
    bsz, seq, d = x.shape
    h = x.reshape(bsz * seq, d)
    depth = ln_mix_g.shape[0]
    for i in range(depth):
        j = i // 2
        if i % 2 == 0:
            h = _layer_diff_moba(h, bsz, seq, a_w_in[j], a_lam_q1[j], a_lam_k1[j], a_lam_q2[j], a_lam_k2[j],
                                 a_subln_g[j], a_w_out[j], ln_mix_g[i], ln_mix_b[i], i)
        else:
            h = _layer_nsa(h, bsz, seq, c_w_in[j], c_pe_k[j], c_w1_k[j], c_w2_k[j], c_pe_v[j], c_w1_v[j],
                           c_w2_v[j], c_w_out[j], ln_mix_g[i], ln_mix_b[i])
        h = _moe_ln(h, i, router_w[i], router_b[i], w_gu, w_down, ws_gu[i], ws_down[i],
                    ln_ffn_g[i], ln_ffn_b[i])
    return h.reshape(bsz, seq, d)
```

```python
import functools
import math

import jax
import jax.numpy as jnp
import numpy as np
from jax import lax
from jax.experimental import pallas as pl
from jax.experimental.pallas import tpu as pltpu

F32 = jnp.float32
BF16 = jnp.bfloat16
I32 = jnp.int32

HEAD_DIM = 128
ROPE_THETA = 10000.0
NEG = -1e30
LN_EPS = 1e-5
DEPTH = 2
DN_ALPHA = (2 * DEPTH) ** 0.25

DIFF_HEADS = 8
MOBA_HEADS = 8
MOBA_BLOCK = 256
MOBA_TOPK = 3
NSA_HEADS = 16
NSA_GROUPS = 2
NSA_REP = NSA_HEADS // NSA_GROUPS
NSA_CMP_LEN = 32
NSA_CMP_STRIDE = 16
NSA_SEL_BLOCK = 64
NSA_SEL_N = 16
NSA_WINDOW = 512
NSA_FORCED = 1e4
N_EXPERTS = 64
EXPERT_DIM = 512
TOP_K = 8
N_GROUPS = 8
GROUP_SIZE = N_EXPERTS // N_GROUPS
TOPK_GROUPS = 4
ROUTED_SCALE = 2.5

LANES = 128
ISSUE_GROUP = 16
VMEM_LIMIT = 56 * 1024 * 1024

_NT = (((1,), (1,)), ((), ()))


def _cp(sem, **kw):
    return pltpu.CompilerParams(dimension_semantics=sem, vmem_limit_bytes=VMEM_LIMIT, **kw)


def _layernorm(x, g, b):
    mu = jnp.mean(x, axis=-1, keepdims=True)
    xc = x - mu
    var = jnp.mean(xc * xc, axis=-1, keepdims=True)
    return xc * lax.rsqrt(var + LN_EPS) * g + b


def _mm_kernel(x_ref, w_ref, o_ref):
    o_ref[...] = jnp.dot(x_ref[...].astype(BF16), w_ref[...],
                         preferred_element_type=F32).astype(o_ref.dtype)


def _matmul(x, w, tm, tn, out_dtype=F32):
    m, k = x.shape
    n = w.shape[1]
    tm, tn = min(tm, m), min(tn, n)
    return pl.pallas_call(
        _mm_kernel,
        grid=(m // tm, n // tn),
        in_specs=[pl.BlockSpec((tm, k), lambda i, j: (i, 0)),
                  pl.BlockSpec((k, tn), lambda i, j: (0, j))],
        out_specs=pl.BlockSpec((tm, tn), lambda i, j: (i, j)),
        out_shape=jax.ShapeDtypeStruct((m, n), out_dtype),
        compiler_params=_cp(("parallel", "arbitrary")),
        name="proj_in",
    )(x, w)


def _proj_ln_kernel(*refs, n_in, alpha):
    a_refs, w_refs = refs[:n_in], refs[n_in:2 * n_in]
    h_ref, g_ref, b_ref, o_ref = refs[2 * n_in:]
    acc = alpha * h_ref[...]
    for a_ref, w_ref in zip(a_refs, w_refs):
        acc = acc + jnp.dot(a_ref[...], w_ref[...], preferred_element_type=F32)
    o_ref[...] = _layernorm(acc, g_ref[...], b_ref[...])


def _proj_ln(acts, ws, h, g, b, tm=256):
    m, d = h.shape
    tm = min(tm, m)
    n_in = len(acts)
    in_specs = ([pl.BlockSpec((tm, a.shape[1]), lambda i: (i, 0)) for a in acts]
                + [pl.BlockSpec(w.shape, lambda i: (0, 0)) for w in ws]
                + [pl.BlockSpec((tm, d), lambda i: (i, 0)),
                   pl.BlockSpec((1, d), lambda i: (0, 0)),
                   pl.BlockSpec((1, d), lambda i: (0, 0))])
    return pl.pallas_call(
        functools.partial(_proj_ln_kernel, n_in=n_in, alpha=DN_ALPHA),
        grid=(m // tm,),
        in_specs=in_specs,
        out_specs=pl.BlockSpec((tm, d), lambda i: (i, 0)),
        out_shape=jax.ShapeDtypeStruct((m, d), F32),
        compiler_params=_cp(("parallel",)),
        name="proj_out_ln",
    )(*acts, *ws, h, g.reshape(1, d), b.reshape(1, d))


def _rope_tables(seq, d):
    half = d // 2
    inv = ROPE_THETA ** (-jnp.arange(half, dtype=F32) * 2.0 / d)
    ang = jnp.arange(seq, dtype=F32)[:, None] * inv[None, :]
    cos, sin = jnp.cos(ang), jnp.sin(ang)
    return jnp.concatenate([cos, cos], axis=1), jnp.concatenate([-sin, sin], axis=1)


def _rope128(x, cos, sin_signed):
    return x * cos + pltpu.roll(x, 64, 1) * sin_signed


def _rope64x2(x, cos, sin_lo, sin_hi):
    return x * cos + pltpu.roll(x, 96, 1) * sin_lo + pltpu.roll(x, 32, 1) * sin_hi


def _split0_kernel(p_ref, c64_ref, slo_ref, shi_ref, c128_ref, s128_ref,
                   dq_ref, dk_ref, dv_ref, mq_ref, mk_ref, mv_ref, km_ref, *, rows):
    c64, slo, shi = c64_ref[...], slo_ref[...], shi_ref[...]
    c128, s128 = c128_ref[...], s128_ref[...]
    w = DIFF_HEADS * HEAD_DIM
    dscale = (HEAD_DIM // 2) ** -0.5
    mscale = HEAD_DIM ** -0.5
    for h in range(DIFF_HEADS):
        lo, hi = h * HEAD_DIM, (h + 1) * HEAD_DIM
        dq_ref[:, lo:hi] = (_rope64x2(p_ref[:, lo:hi], c64, slo, shi) * dscale).astype(BF16)
        dk_ref[:, lo:hi] = _rope64x2(p_ref[:, w + lo:w + hi], c64, slo, shi).astype(BF16)
        dv_ref[:, lo:hi] = p_ref[:, 2 * w + lo:2 * w + hi].astype(BF16)
        mq_ref[:, lo:hi] = (_rope128(p_ref[:, 3 * w + lo:3 * w + hi], c128, s128) * mscale).astype(BF16)
        kr = _rope128(p_ref[:, 4 * w + lo:4 * w + hi], c128, s128)
        mk_ref[:, lo:hi] = kr.astype(BF16)
        km_ref[0, :, lo:hi] = jnp.sum(kr, axis=0, keepdims=True) * (1.0 / rows)
        mv_ref[:, lo:hi] = p_ref[:, 5 * w + lo:5 * w + hi].astype(BF16)


def _split0(p, seq):
    t = p.shape[0]
    rows = MOBA_BLOCK
    w = DIFF_HEADS * HEAD_DIM
    nb = seq // rows
    c64h, s64h = _rope_tables(seq, HEAD_DIM // 2)
    c64 = jnp.concatenate([c64h, c64h], axis=1)
    lane = jnp.arange(HEAD_DIM) % (HEAD_DIM // 2)
    s64 = jnp.concatenate([s64h, s64h], axis=1)
    slo = jnp.where(lane < HEAD_DIM // 4, s64, 0.0)
    shi = jnp.where(lane >= HEAD_DIM // 4, s64, 0.0)
    c128, s128 = _rope_tables(seq, HEAD_DIM)
    tab = pl.BlockSpec((rows, HEAD_DIM), lambda i: (i % nb, 0))
    out = pl.BlockSpec((rows, w), lambda i: (i, 0))
    bf = jax.ShapeDtypeStruct((t, w), BF16)
    return pl.pallas_call(
        functools.partial(_split0_kernel, rows=rows),
        grid=(t // rows,),
        in_specs=[pl.BlockSpec((rows, 6 * w), lambda i: (i, 0)), tab, tab, tab, tab, tab],
        out_specs=[out] * 6 + [pl.BlockSpec((1, 1, w), lambda i: (i, 0, 0))],
        out_shape=[bf] * 6 + [jax.ShapeDtypeStruct((t // rows, 1, w), F32)],
        compiler_params=_cp(("parallel",)),
        name="split_rope_l0",
    )(p, c64, slo, shi, c128, s128)


def _flash_update(carry, s, v):
    m, l, acc = carry
    m_new = jnp.maximum(m, jnp.max(s, axis=1, keepdims=True))
    p = jnp.exp(s - m_new)
    a = jnp.exp(m - m_new)
    l = a * l + jnp.sum(p, axis=1, keepdims=True)
    acc = a * acc + jnp.dot(p.astype(BF16), v, preferred_element_type=F32)
    return m_new, l, acc


def _flash_init(rows, d):
    return (jnp.full((rows, 1), NEG, F32), jnp.zeros((rows, 1), F32), jnp.zeros((rows, d), F32))


def _diff_kernel(lam_ref, q_ref, k_ref, v_ref, g_ref, o_ref, *, tq, out_scale):
    i = pl.program_id(2)
    q = q_ref[0]
    lane = lax.broadcasted_iota(I32, (tq, HEAD_DIM), 1)
    zero = jnp.zeros_like(q)
    qs = jnp.concatenate([jnp.where(lane < HEAD_DIM // 2, q, zero),
                          jnp.where(lane >= HEAD_DIM // 2, q, zero)], axis=0)

    def tile(j):
        st = pl.multiple_of(j * tq, tq)
        k = k_ref[0, pl.ds(st, tq), :]
        v = v_ref[0, pl.ds(st, tq), :]
        return lax.dot_general(qs, k, _NT, preferred_element_type=F32), v

    def body(j, carry):
        s, v = tile(j)
        return _flash_update(carry, s, v)

    carry = lax.fori_loop(0, i, body, _flash_init(2 * tq, HEAD_DIM))
    s, v = tile(i)
    r = lax.broadcasted_iota(I32, (2 * tq, tq), 0)
    c = lax.broadcasted_iota(I32, (2 * tq, tq), 1)
    s = jnp.where(c <= jnp.where(r >= tq, r - tq, r), s, NEG)
    _, l, acc = _flash_update(carry, s, v)
    o = acc / l
    od = o[:tq] - lam_ref[0] * o[tq:]
    od = od * lax.rsqrt(jnp.mean(od * od, axis=1, keepdims=True) + LN_EPS) * g_ref[...] * out_scale
    o_ref[0] = od.astype(o_ref.dtype)


def _diff_attention(lam, q, k, v, subln_g, lam_init, tq=1024):
    b, s, _ = q.shape
    tq = min(tq, s)
    kv = pl.BlockSpec((1, s, HEAD_DIM), lambda bi, h, i, *_: (bi, 0, h))
    qo = pl.BlockSpec((1, tq, HEAD_DIM), lambda bi, h, i, *_: (bi, i, h))
    return pl.pallas_call(
        functools.partial(_diff_kernel, tq=tq, out_scale=1.0 - lam_init),
        grid_spec=pltpu.PrefetchScalarGridSpec(
            num_scalar_prefetch=1,
            grid=(b, DIFF_HEADS, s // tq),
            in_specs=[qo, kv, kv, pl.BlockSpec((1, HEAD_DIM), lambda bi, h, i, *_: (0, 0))],
            out_specs=qo),
        out_shape=jax.ShapeDtypeStruct(q.shape, BF16),
        compiler_params=_cp(("parallel", "parallel", "arbitrary")),
        name="diff_attention",
    )(lam, q, k, v, subln_g.reshape(1, HEAD_DIM))


def _moba_kernel(q_ref, k_ref, v_ref, km_ref, o_ref, *, tq, blk, nb, n_pick, hp):
    i = pl.program_id(2)
    bid = lax.broadcasted_iota(I32, (tq, nb), 1)
    own = (i * tq + lax.broadcasted_iota(I32, (tq, nb), 0)) // blk
    past = bid < own
    own_col = (i * tq + lax.broadcasted_iota(I32, (tq, 1), 0)) // blk
    rpos = lax.broadcasted_iota(I32, (tq, blk), 0) & (blk - 1)
    kcol = lax.broadcasted_iota(I32, (tq, blk), 1)
    causal = kcol <= rpos
    heads = [slice(h * HEAD_DIM, (h + 1) * HEAD_DIM) for h in range(hp)]

    qs, sels = [], []
    for hs in heads:
        q = q_ref[0, :, hs]
        gate = lax.dot_general(q.astype(F32), km_ref[0, :, hs], _NT, precision=lax.Precision.HIGHEST,
                               preferred_element_type=F32)
        g = jnp.where(past, gate, NEG)
        sel = jnp.zeros((tq, nb), F32)
        for _ in range(n_pick):
            mx = jnp.max(g, axis=1, keepdims=True)
            idx = jnp.min(jnp.where(g == mx, bid, nb), axis=1, keepdims=True)
            pick = bid == idx
            sel = jnp.where(pick, jnp.where(past, 1.0, 0.0), sel)
            g = jnp.where(pick, -jnp.inf, g)
        qs.append(q)
        sels.append(sel)

    def body(j, carries):
        st = pl.multiple_of(j * blk, blk)
        out = []
        for hs, q, sel, carry in zip(heads, qs, sels, carries):
            s = lax.dot_general(q, k_ref[0, pl.ds(st, blk), hs], _NT, preferred_element_type=F32)
            picked = jnp.sum(jnp.where(bid == j, sel, 0.0), axis=1, keepdims=True) > 0.5
            keep = picked | ((own_col == j) & causal)
            out.append(_flash_update(carry, jnp.where(keep, s, NEG), v_ref[0, pl.ds(st, blk), hs]))
        return tuple(out)

    n_blocks = (i + 1) * (tq // blk)
    carries = lax.fori_loop(0, n_blocks, body, tuple(_flash_init(tq, HEAD_DIM) for _ in heads))
    for hs, (_, l, acc) in zip(heads, carries):
        o_ref[0, :, hs] = (acc / l).astype(o_ref.dtype)


def _moba_attention(q, k, v, kmean, tq=1024, hp=2):
    b, s, _ = q.shape
    blk = min(MOBA_BLOCK, s)
    tq = min(tq, s)
    nb = s // blk
    n_pick = max(1, min(MOBA_TOPK, nb - 1))
    kv = pl.BlockSpec((1, s, hp * HEAD_DIM), lambda bi, h, i: (bi, 0, h))
    qo = pl.BlockSpec((1, tq, hp * HEAD_DIM), lambda bi, h, i: (bi, i, h))
    return pl.pallas_call(
        functools.partial(_moba_kernel, tq=tq, blk=blk, nb=nb, n_pick=n_pick, hp=hp),
        grid=(b, MOBA_HEADS // hp, s // tq),
        in_specs=[qo, kv, kv, pl.BlockSpec((1, nb, hp * HEAD_DIM), lambda bi, h, i: (bi, 0, h))],
        out_specs=qo,
        out_shape=jax.ShapeDtypeStruct(q.shape, BF16),
        compiler_params=_cp(("parallel", "parallel", "arbitrary")),
        name="moba_attention",
    )(q, k, v, kmean)


def _split1_kernel(p_ref, c_ref, s_ref, q_ref, qr_ref, ks_ref, vs_ref, kw_ref, vw_ref, gt_ref):
    cos, sin = c_ref[...], s_ref[...]
    scale = HEAD_DIM ** -0.5
    wq = NSA_HEADS * HEAD_DIM
    wkv = NSA_GROUPS * HEAD_DIM
    for h in range(NSA_HEADS):
        lo, hi = h * HEAD_DIM, (h + 1) * HEAD_DIM
        x = p_ref[:, lo:hi]
        q_ref[:, lo:hi] = (x * scale).astype(BF16)
        qr_ref[:, lo:hi] = (_rope128(x, cos, sin) * scale).astype(BF16)
    base = wq + 2 * wkv
    for g in range(NSA_GROUPS):
        lo, hi = g * HEAD_DIM, (g + 1) * HEAD_DIM
        ks_ref[:, lo:hi] = _rope128(p_ref[:, base + lo:base + hi], cos, sin).astype(BF16)
        vs_ref[:, lo:hi] = p_ref[:, base + wkv + lo:base + wkv + hi].astype(BF16)
        kw_ref[:, lo:hi] = _rope128(p_ref[:, base + 2 * wkv + lo:base + 2 * wkv + hi], cos, sin).astype(BF16)
        vw_ref[:, lo:hi] = p_ref[:, base + 3 * wkv + lo:base + 3 * wkv + hi].astype(BF16)
    gt_ref[...] = jax.nn.sigmoid(p_ref[:, base + 4 * wkv:base + 4 * wkv + LANES])


def _split1(p, seq, rows=256):
    t, wp = p.shape
    rows = min(rows, seq)
    nb = seq // rows
    wq = NSA_HEADS * HEAD_DIM
    wkv = NSA_GROUPS * HEAD_DIM
    cos, sin = _rope_tables(seq, HEAD_DIM)
    tab = pl.BlockSpec((rows, HEAD_DIM), lambda i: (i % nb, 0))
    oq = pl.BlockSpec((rows, wq), lambda i: (i, 0))
    okv = pl.BlockSpec((rows, wkv), lambda i: (i, 0))
    sq = jax.ShapeDtypeStruct((t, wq), BF16)
    skv = jax.ShapeDtypeStruct((t, wkv), BF16)
    return pl.pallas_call(
        _split1_kernel,
        grid=(t // rows,),
        in_specs=[pl.BlockSpec((rows, wp), lambda i: (i, 0)), tab, tab],
        out_specs=[oq, oq, okv, okv, okv, okv, pl.BlockSpec((rows, LANES), lambda i: (i, 0))],
        out_shape=[sq, sq, skv, skv, skv, skv, jax.ShapeDtypeStruct((t, LANES), F32)],
        compiler_params=_cp(("parallel",)),
        name="split_rope_l1",
    )(p, cos, sin)


def _gelu_tanh(x):
    return 0.5 * x * (1.0 + jnp.tanh(math.sqrt(2.0 / math.pi) * (x + 0.044715 * (x * x * x))))


def _compress_kernel(c_ref, pelo_ref, pehi_ref, w1lo_ref, w1hi_ref, w2_ref, o_ref, *, nc):
    c = c_ref[0]
    a = jnp.dot((c + pelo_ref[...]).astype(BF16), w1lo_ref[...], preferred_element_type=F32)
    bm = jnp.dot((c + pehi_ref[...]).astype(BF16), w1hi_ref[...], preferred_element_type=F32)
    hid = _gelu_tanh(a + pltpu.roll(bm, nc - 1, 0))
    o_ref[0] = jnp.dot(hid.astype(BF16), w2_ref[...], preferred_element_type=F32).astype(o_ref.dtype)


def _compress(chunks, pe, w1, w2):
    bg, nc, cw = chunks.shape
    half = NSA_CMP_STRIDE
    pelo = pe[:half].reshape(1, cw)
    pehi = pe[half:].reshape(1, cw)
    w1lo = w1[:cw].astype(BF16)
    w1hi = w1[cw:].astype(BF16)
    full = lambda a: pl.BlockSpec(a.shape, lambda i: (0,) * a.ndim)
    return pl.pallas_call(
        functools.partial(_compress_kernel, nc=nc),
        grid=(bg,),
        in_specs=[pl.BlockSpec((1, nc, cw), lambda i: (i, 0, 0)), full(pelo), full(pehi),
                  full(w1lo), full(w1hi), pl.BlockSpec(w2.shape, lambda i: (0, 0))],
        out_specs=pl.BlockSpec((1, nc, HEAD_DIM), lambda i: (i, 0, 0)),
        out_shape=jax.ShapeDtypeStruct((bg, nc, HEAD_DIM), BF16),
        compiler_params=_cp(("parallel",)),
        name="nsa_compress",
    )(chunks, pelo, pehi, w1lo, w1hi, w2.astype(BF16))


def _stack_heads(q):
    return jnp.concatenate([q[:, r * HEAD_DIM:(r + 1) * HEAD_DIM] for r in range(NSA_REP)], axis=0)


def _unstack_store(o_ref, o, tq):
    for r in range(NSA_REP):
        o_ref[0, :, r * HEAD_DIM:(r + 1) * HEAD_DIM] = o[r * tq:(r + 1) * tq].astype(o_ref.dtype)


def _cmp_select_kernel(q_ref, kc_ref, vc_ref, o_ref, sel_ref, *, tq, nc, nblk, n_pick):
    i = pl.program_id(2)
    qs = _stack_heads(q_ref[0])
    s = lax.dot_general(qs, kc_ref[0], _NT, preferred_element_type=F32)
    rows = NSA_REP * tq
    row = lax.broadcasted_iota(I32, (rows, nc), 0)
    n = lax.broadcasted_iota(I32, (rows, nc), 1)
    pos = i * tq + (row & (tq - 1))
    valid = n * NSA_CMP_STRIDE + (NSA_CMP_LEN - 1) <= pos
    sm = jnp.where(valid, s, NEG)
    e = jnp.exp(sm - jnp.max(sm, axis=1, keepdims=True))
    p = jnp.where(valid, e / jnp.sum(e, axis=1, keepdims=True), 0.0)
    o = jnp.dot(p.astype(BF16), vc_ref[0], preferred_element_type=F32)
    _unstack_store(o_ref, o, tq)

    psum = p[0:tq]
    for r in range(1, NSA_REP):
        psum = psum + p[r * tq:(r + 1) * tq]
    cn = lax.broadcasted_iota(I32, (nc, nblk), 0) * NSA_CMP_STRIDE
    bj = lax.broadcasted_iota(I32, (nc, nblk), 1) * NSA_SEL_BLOCK
    ov = jnp.maximum(jnp.minimum(cn + NSA_CMP_LEN, bj + NSA_SEL_BLOCK) - jnp.maximum(cn, bj), 0)
    ov = ov.astype(F32) * (1.0 / NSA_CMP_LEN)
    imp = jnp.dot(psum, ov, precision=lax.Precision.HIGHEST, preferred_element_type=F32)

    cur = (i * tq + lax.broadcasted_iota(I32, (tq, nblk), 0)) // NSA_SEL_BLOCK
    blk = lax.broadcasted_iota(I32, (tq, nblk), 1)
    allowed = blk <= cur
    forced = (blk == 0) | (blk == cur) | (blk == cur - 1)
    val = jnp.where(allowed, jnp.where(forced, NSA_FORCED, imp), NEG)
    sel = jnp.zeros((tq, nblk), F32)
    for _ in range(n_pick):
        mx = jnp.max(val, axis=1, keepdims=True)
        idx = jnp.min(jnp.where(val == mx, blk, nblk), axis=1, keepdims=True)
        pick = blk == idx
        sel = jnp.where(pick, 1.0, sel)
        val = jnp.where(pick, -jnp.inf, val)
    sel_ref[0, 0] = jnp.where(allowed, sel, 0.0)


def _cmp_select(q, kcmp, vcmp, tq=512):
    b, s, _ = q.shape
    tq = min(tq, s)
    nc = kcmp.shape[1]
    nblk = s // NSA_SEL_BLOCK
    n_pick = min(NSA_SEL_N, nblk)
    wg = NSA_REP * HEAD_DIM
    qo = pl.BlockSpec((1, tq, wg), lambda bi, g, i: (bi, i, g))
    kv = pl.BlockSpec((1, nc, HEAD_DIM), lambda bi, g, i: (bi * NSA_GROUPS + g, 0, 0))
    return pl.pallas_call(
        functools.partial(_cmp_select_kernel, tq=tq, nc=nc, nblk=nblk, n_pick=n_pick),
        grid=(b, NSA_GROUPS, s // tq),
        in_specs=[qo, kv, kv],
        out_specs=[qo, pl.BlockSpec((1, 1, tq, nblk), lambda bi, g, i: (bi, g, i, 0))],
        out_shape=[jax.ShapeDtypeStruct(q.shape, F32),
                   jax.ShapeDtypeStruct((b, NSA_GROUPS, s, nblk), F32)],
        compiler_params=_cp(("parallel", "parallel", "arbitrary")),
        name="nsa_cmp_select",
    )(q, kcmp, vcmp)


def _mask_heads(s, keep, tq):
    tk = s.shape[1]
    s3 = jnp.where(keep[None], s.reshape(NSA_REP, tq, tk), NEG)
    return s3.reshape(NSA_REP * tq, tk)


def _sel_kernel(q_ref, k_ref, v_ref, sel_ref, o_ref, *, tq, tk, nblk):
    i = pl.program_id(2)
    qs = _stack_heads(q_ref[0])
    sel = sel_ref[0, 0].astype(BF16)
    per = tk // NSA_SEL_BLOCK
    qpos = i * tq + lax.broadcasted_iota(I32, (tq, tk), 0)
    kcol = lax.broadcasted_iota(I32, (tq, tk), 1)
    eb = lax.broadcasted_iota(I32, (nblk, tk), 0)
    ec = lax.broadcasted_iota(I32, (nblk, tk), 1) // NSA_SEL_BLOCK

    def body(j, carry):
        st = pl.multiple_of(j * tk, tk)
        k = k_ref[0, pl.ds(st, tk), :]
        v = v_ref[0, pl.ds(st, tk), :]
        s = lax.dot_general(qs, k, _NT, preferred_element_type=F32)
        expand = jnp.where(eb == j * per + ec, 1.0, 0.0).astype(BF16)
        on = jnp.dot(sel, expand, preferred_element_type=F32)
        keep = jnp.where(j * tk + kcol <= qpos, on, 0.0) > 0.5
        return _flash_update(carry, _mask_heads(s, keep, tq), v)

    n_tiles = (i * tq + tq - 1) // tk + 1
    _, l, acc = lax.fori_loop(0, n_tiles, body, _flash_init(NSA_REP * tq, HEAD_DIM))
    _unstack_store(o_ref, acc / l, tq)


def _sel_attention(q, k, v, sel, tq=256, tk=512):
    b, s, _ = q.shape
    tq, tk = min(tq, s), min(tk, s)
    nblk = s // NSA_SEL_BLOCK
    wg = NSA_REP * HEAD_DIM
    qo = pl.BlockSpec((1, tq, wg), lambda bi, g, i: (bi, i, g))
    kv = pl.BlockSpec((1, s, HEAD_DIM), lambda bi, g, i: (bi, 0, g))
    return pl.pallas_call(
        functools.partial(_sel_kernel, tq=tq, tk=tk, nblk=nblk),
        grid=(b, NSA_GROUPS, s // tq),
        in_specs=[qo, kv, kv, pl.BlockSpec((1, 1, tq, nblk), lambda bi, g, i: (bi, g, i, 0))],
        out_specs=qo,
        out_shape=jax.ShapeDtypeStruct(q.shape, F32),
        compiler_params=_cp(("parallel", "parallel", "arbitrary")),
        name="nsa_sel_attention",
    )(q, k, v, sel)


def _win_kernel(q_ref, k_ref, v_ref, oc_ref, os_ref, gt_ref, o_ref, *, tq, window):
    i = pl.program_id(2)
    qs = _stack_heads(q_ref[0])
    qpos = i * tq + lax.broadcasted_iota(I32, (tq, tq), 0)
    kcol = lax.broadcasted_iota(I32, (tq, tq), 1)

    def body(j, carry):
        st = pl.multiple_of(j * tq, tq)
        k = k_ref[0, pl.ds(st, tq), :]
        v = v_ref[0, pl.ds(st, tq), :]
        s = lax.dot_general(qs, k, _NT, preferred_element_type=F32)
        kpos = j * tq + kcol
        keep = (kpos <= qpos) & (kpos > qpos - window)
        return _flash_update(carry, _mask_heads(s, keep, tq), v)

    def rev_body(t, carry):
        return body(i - t, carry)

    n_tiles = jnp.minimum(i, window // tq) + 1
    _, l, acc = lax.fori_loop(0, n_tiles, rev_body, _flash_init(NSA_REP * tq, HEAD_DIM))
    ow = acc / l
    gt = gt_ref[0]
    for r in range(NSA_REP):
        lo, hi = r * HEAD_DIM, (r + 1) * HEAD_DIM
        mix = (gt[:, 3 * r:3 * r + 1] * oc_ref[0, :, lo:hi]
               + gt[:, 3 * r + 1:3 * r + 2] * os_ref[0, :, lo:hi]
               + gt[:, 3 * r + 2:3 * r + 3] * ow[r * tq:(r + 1) * tq])
        o_ref[0, :, lo:hi] = mix.astype(o_ref.dtype)


def _win_merge(q, k, v, o_cmp, o_sel, gates, tq=256):
    b, s, _ = q.shape
    tq = min(tq, s)
    wg = NSA_REP * HEAD_DIM
    qo = pl.BlockSpec((1, tq, wg), lambda bi, g, i: (bi, i, g))
    kv = pl.BlockSpec((1, s, HEAD_DIM), lambda bi, g, i: (bi, 0, g))
    return pl.pallas_call(
        functools.partial(_win_kernel, tq=tq, window=NSA_WINDOW),
        grid=(b, NSA_GROUPS, s // tq),
        in_specs=[qo, kv, kv, qo, qo, pl.BlockSpec((1, tq, LANES), lambda bi, g, i: (bi, i, g))],
        out_specs=qo,
        out_shape=jax.ShapeDtypeStruct(q.shape, BF16),
        compiler_params=_cp(("parallel", "parallel", "arbitrary")),
        name="nsa_window_merge",
    )(q, k, v, o_cmp, o_sel, gates)


def _router_kernel(x_ref, w_ref, b_ref, idx_ref, wt_ref):
    tm = x_ref.shape[0]
    logits = jnp.dot(x_ref[...], w_ref[...], precision=lax.Precision.HIGHEST,
                     preferred_element_type=F32)
    score = jax.nn.sigmoid(logits.T[:N_EXPERTS])
    biased = score + b_ref[...]
    eid = lax.broadcasted_iota(I32, (N_EXPERTS, tm), 0)
    grp = eid // GROUP_SIZE

    b3 = biased.reshape(N_GROUPS, GROUP_SIZE, tm)
    e3 = eid.reshape(N_GROUPS, GROUP_SIZE, tm)
    m1 = jnp.max(b3, axis=1, keepdims=True)
    i1 = jnp.min(jnp.where(b3 == m1, e3, N_EXPERTS), axis=1, keepdims=True)
    m2 = jnp.max(jnp.where(e3 == i1, -jnp.inf, b3), axis=1, keepdims=True)
    gscore = jnp.broadcast_to(m1 + m2, (N_GROUPS, GROUP_SIZE, tm)).reshape(N_EXPERTS, tm)

    emask = jnp.zeros((N_EXPERTS, tm), F32)
    for _ in range(TOPK_GROUPS):
        mx = jnp.max(gscore, axis=0, keepdims=True)
        gi = jnp.min(jnp.where(gscore == mx, grp, N_GROUPS), axis=0, keepdims=True)
        pick = grp == gi
        emask = jnp.where(pick, 1.0, emask)
        gscore = jnp.where(pick, -jnp.inf, gscore)

    val = jnp.where(emask > 0.5, biased, NEG)
    krow = lax.broadcasted_iota(I32, (TOP_K, tm), 0)
    idx_out = jnp.zeros((TOP_K, tm), I32)
    wt_out = jnp.zeros((TOP_K, tm), F32)
    for kk in range(TOP_K):
        mx = jnp.max(val, axis=0, keepdims=True)
        ei = jnp.min(jnp.where(val == mx, eid, N_EXPERTS), axis=0, keepdims=True)
        pick = eid == ei
        wk = jnp.sum(jnp.where(pick, score, 0.0), axis=0, keepdims=True)
        idx_out = jnp.where(krow == kk, ei, idx_out)
        wt_out = jnp.where(krow == kk, wk, wt_out)
        val = jnp.where(pick, -jnp.inf, val)
    idx_ref[...] = idx_out
    wt_ref[...] = wt_out / jnp.sum(wt_out, axis=0, keepdims=True) * ROUTED_SCALE


def _router(x, w, bias, tm=512):
    t, d = x.shape
    tm = min(tm, t)
    idx, wts = pl.pallas_call(
        _router_kernel,
        grid=(t // tm,),
        in_specs=[pl.BlockSpec((tm, d), lambda i: (i, 0)),
                  pl.BlockSpec((d, LANES), lambda i: (0, 0)),
                  pl.BlockSpec((N_EXPERTS, 1), lambda i: (0, 0))],
        out_specs=[pl.BlockSpec((TOP_K, tm), lambda i: (0, i)),
                   pl.BlockSpec((TOP_K, tm), lambda i: (0, i))],
        out_shape=[jax.ShapeDtypeStruct((TOP_K, t), I32), jax.ShapeDtypeStruct((TOP_K, t), F32)],
        compiler_params=_cp(("parallel",)),
        name="moe_router",
    )(x, jnp.pad(w, ((0, 0), (0, LANES - N_EXPERTS))), bias.reshape(N_EXPERTS, 1))
    return idx.T, wts.T


def _moe_plan(eidx, tm):
    t = eidx.shape[0]
    n_slots = t * TOP_K
    flat_e = eidx.reshape(-1)
    order = jnp.argsort(flat_e, stable=True).astype(I32)
    rank = jnp.argsort(order).astype(I32)
    experts = jnp.arange(N_EXPERTS, dtype=I32)
    counts = jnp.sum((flat_e[:, None] == experts[None, :]).astype(I32), axis=0)
    pcounts = (counts + tm - 1) // tm * tm
    pend = jnp.cumsum(pcounts)
    pstart = pend - pcounts
    cstart = jnp.cumsum(counts) - counts
    n_rows = n_slots + N_EXPERTS * tm
    n_tiles = n_rows // tm
    tile_start = jnp.arange(n_tiles, dtype=I32) * tm
    tile_e = jnp.clip(jnp.sum((pend[None, :] <= tile_start[:, None]).astype(I32), axis=1), 0, N_EXPERTS - 1)
    j = tile_start[:, None] + jnp.arange(tm, dtype=I32)[None, :] - pstart[tile_e][:, None]
    src = jnp.clip(cstart[tile_e][:, None] + j, 0, n_slots - 1)
    row_tok = jnp.where(j < counts[tile_e][:, None], order[src] // TOP_K, 0).astype(I32)
    slot_row = ((pstart - cstart)[flat_e] + rank).astype(I32)
    n_used = (pend[-1] // tm).astype(I32).reshape(1)
    return row_tok, slot_row, tile_e.astype(I32), n_used


def _gather_rows_pipelined(i, n_steps, idx_hbm, src_hbm, idx_smem, buf, idx_sem, row_sem, rows):
    slot = i % 2

    def idx_copy(t, s):
        return pltpu.make_async_copy(idx_hbm.at[t], idx_smem[s], idx_sem.at[s])

    def issue_rows(s):
        def issue(g, c):
            for k in range(ISSUE_GROUP):
                pltpu.make_async_copy(src_hbm.at[pl.ds(idx_smem[s][g * ISSUE_GROUP + k], 1)],
                                      buf.at[s, g * (ISSUE_GROUP // 8) + k // 8, pl.ds(k % 8, 1)],
                                      row_sem.at[s]).start()
            return c
        lax.fori_loop(0, rows // ISSUE_GROUP, issue, 0)

    @pl.when(i == 0)
    def _():
        idx_copy(0, 0).start()
        idx_copy(0, 0).wait()
        issue_rows(0)

        @pl.when(n_steps > 1)
        def _():
            idx_copy(1, 1).start()

    for s in range(2):
        @pl.when(jnp.logical_and(i + 1 < n_steps, slot == 1 - s))
        def _():
            idx_copy(i + 1, s).wait()
            issue_rows(s)

            @pl.when(i + 2 < n_steps)
            def _():
                idx_copy(i + 2, 1 - s).start()

    @pl.when(i < n_steps)
    def _():
        pltpu.make_async_copy(buf.at[slot], buf.at[slot], row_sem.at[slot]).wait()


def _expert_kernel(te_ref, nused_ref, tok_hbm, x_hbm, wgu_ref, wd_ref, o_ref,
                   idx0, idx1, xbuf, idx_sem, row_sem, *, tm):
    i = pl.program_id(0)
    n_used = nused_ref[0]
    d = x_hbm.shape[1]
    _gather_rows_pipelined(i, n_used, tok_hbm, x_hbm, (idx0, idx1), xbuf, idx_sem, row_sem, tm)

    @pl.when(i < n_used)
    def _():
        gu = jnp.dot(xbuf[i % 2].reshape(tm, d).astype(BF16), wgu_ref[0, 0].astype(BF16),
                     preferred_element_type=F32)
        act = jax.nn.silu(gu[:, :EXPERT_DIM]) * gu[:, EXPERT_DIM:]
        o_ref[...] = jnp.dot(act.astype(BF16), wd_ref[0, 0].astype(BF16), preferred_element_type=F32)

    @pl.when(i >= n_used)
    def _():
        o_ref[...] = jnp.zeros_like(o_ref)


def _expert_ffn(x, row_tok, w_gu, w_down, layer, tile_e, n_used):
    n_tiles, tm = row_tok.shape
    d = x.shape[1]
    last = lambda i, te, nu: jnp.minimum(i, nu[0] - 1)
    return pl.pallas_call(
        functools.partial(_expert_kernel, tm=tm),
        grid_spec=pltpu.PrefetchScalarGridSpec(
            num_scalar_prefetch=2,
            grid=(n_tiles,),
            in_specs=[pl.BlockSpec(memory_space=pl.ANY), pl.BlockSpec(memory_space=pl.ANY),
                      pl.BlockSpec((1, 1, d, 2 * EXPERT_DIM),
                                   lambda i, te, nu: (layer, te[last(i, te, nu)], 0, 0)),
                      pl.BlockSpec((1, 1, EXPERT_DIM, d),
                                   lambda i, te, nu: (layer, te[last(i, te, nu)], 0, 0))],
            out_specs=pl.BlockSpec((tm, d), lambda i, te, nu: (i, 0)),
            scratch_shapes=[pltpu.SMEM((tm,), I32), pltpu.SMEM((tm,), I32), pltpu.VMEM((2, tm // 8, 8, d), F32),
                            pltpu.SemaphoreType.DMA((2,)), pltpu.SemaphoreType.DMA((2,))]),
        out_shape=jax.ShapeDtypeStruct((n_tiles * tm, d), F32),
        compiler_params=_cp(("arbitrary",)),
        name="moe_expert_ffn",
    )(tile_e, n_used, row_tok, x, w_gu, w_down)


def _shared_kernel(x_ref, wgu_ref, wd_ref, o_ref):
    gu = jnp.dot(x_ref[...].astype(BF16), wgu_ref[...], preferred_element_type=F32)
    act = jax.nn.silu(gu[:, :EXPERT_DIM]) * gu[:, EXPERT_DIM:]
    o_ref[...] = jnp.dot(act.astype(BF16), wd_ref[...], preferred_element_type=F32)


def _shared_ffn(x, w_gu, w_down, tm=512):
    t, d = x.shape
    tm = min(tm, t)
    return pl.pallas_call(
        _shared_kernel,
        grid=(t // tm,),
        in_specs=[pl.BlockSpec((tm, d), lambda i: (i, 0)),
                  pl.BlockSpec(w_gu.shape, lambda i: (0, 0)),
                  pl.BlockSpec(w_down.shape, lambda i: (0, 0))],
        out_specs=pl.BlockSpec((tm, d), lambda i: (i, 0)),
        out_shape=jax.ShapeDtypeStruct((t, d), F32),
        compiler_params=_cp(("parallel",)),
        name="moe_shared_ffn",
    )(x, w_gu, w_down)


def _combine_kernel(rows_hbm, ys_hbm, wt_ref, h_ref, sh_ref, g_ref, b_ref, o_ref,
                    idx0, idx1, buf, idx_sem, row_sem, *, tt, steps):
    i = pl.program_id(0)
    d = h_ref.shape[1]
    _gather_rows_pipelined(i, steps, rows_hbm, ys_hbm, (idx0, idx1), buf, idx_sem, row_sem, tt * TOP_K)
    wt = wt_ref[...]
    acc = DN_ALPHA * h_ref[...] + sh_ref[...]
    for k in range(TOP_K):
        yk = buf[i % 2, k * (tt // 8):(k + 1) * (tt // 8)].reshape(tt, d)
        acc = acc + wt[:, k:k + 1] * yk
    o_ref[...] = _layernorm(acc, g_ref[...], b_ref[...])


def _combine_ln(ys, slot_row, wts, h, shared, g, b, tt=64):
    t, d = h.shape
    tt = min(tt, t)
    steps = t // tt
    blk = pl.BlockSpec((tt, d), lambda i: (i, 0))
    vec = pl.BlockSpec((1, d), lambda i: (0, 0))
    return pl.pallas_call(
        functools.partial(_combine_kernel, tt=tt, steps=steps),
        grid=(steps,),
        in_specs=[pl.BlockSpec(memory_space=pl.ANY), pl.BlockSpec(memory_space=pl.ANY),
                  pl.BlockSpec((tt, TOP_K), lambda i: (i, 0)), blk, blk, vec, vec],
        out_specs=blk,
        out_shape=jax.ShapeDtypeStruct((t, d), F32),
        scratch_shapes=[pltpu.SMEM((tt * TOP_K,), I32), pltpu.SMEM((tt * TOP_K,), I32),
                        pltpu.VMEM((2, tt * TOP_K // 8, 8, d), F32),
                        pltpu.SemaphoreType.DMA((2,)), pltpu.SemaphoreType.DMA((2,))],
        compiler_params=_cp(("arbitrary",)),
        name="moe_combine_ln",
    )(slot_row.reshape(steps, tt, TOP_K).transpose(0, 2, 1).reshape(steps, tt * TOP_K),
      ys, wts, h, shared, g.reshape(1, d), b.reshape(1, d))


def _moe_ln(h, layer, router_w, router_b, w_gu, w_down, ws_gu, ws_down, g, b, tm=512):
    eidx, wts = _router(h, router_w, router_b)
    row_tok, slot_row, tile_e, n_used = _moe_plan(eidx, tm)
    ys = _expert_ffn(h, row_tok, w_gu, w_down, layer, tile_e, n_used)
    shared = _shared_ffn(h, ws_gu.astype(BF16), ws_down.astype(BF16))
    return _combine_ln(ys, slot_row, wts, h, shared, g, b)


def _layer_diff_moba(h, bsz, seq, w_in, lam_q1, lam_k1, lam_q2, lam_k2, subln_g, w_out, ln_g, ln_b, layer_idx):
    w = DIFF_HEADS * HEAD_DIM
    p = _matmul(h, w_in.astype(BF16), 1024, 768)
    dq, dk, dv, mq, mk, mv, kmean = _split0(p, seq)
    lam_init = 0.8 - 0.6 * math.exp(-0.3 * layer_idx)
    lam = (jnp.exp(jnp.sum(lam_q1 * lam_k1)) - jnp.exp(jnp.sum(lam_q2 * lam_k2)) + lam_init).reshape(1)
    sh = lambda a: a.reshape(bsz, seq, w)
    o_a = _diff_attention(lam, sh(dq), sh(dk), sh(dv), subln_g, lam_init)
    o_b = _moba_attention(sh(mq), sh(mk), sh(mv), kmean.reshape(bsz, seq // MOBA_BLOCK, w))
    w_out = w_out.astype(BF16)
    return _proj_ln([o_a.reshape(-1, w), o_b.reshape(-1, w)], [w_out[:w], w_out[w:]], h, ln_g, ln_b)


def _layer_nsa(h, bsz, seq, w_in, pe_k, w1_k, w2_k, pe_v, w1_v, w2_v, w_out, ln_g, ln_b):
    wq = NSA_HEADS * HEAD_DIM
    wkv = NSA_GROUPS * HEAD_DIM
    n_in = w_in.shape[1]
    n_pad = -(-(n_in + LANES) // 768) * 768
    p = _matmul(h, jnp.pad(w_in, ((0, 0), (0, n_pad - n_in))).astype(BF16), 1024, 768)
    q, qr, ks, vs, kw, vw, gt = _split1(p, seq)

    def chunks(col):
        a = p[:, col:col + wkv].reshape(bsz, seq, NSA_GROUPS, HEAD_DIM).transpose(0, 2, 1, 3)
        return a.reshape(bsz * NSA_GROUPS, seq // NSA_CMP_STRIDE, NSA_CMP_STRIDE * HEAD_DIM)

    k_cmp = _compress(chunks(wq), pe_k, w1_k, w2_k)
    v_cmp = _compress(chunks(wq + wkv), pe_v, w1_v, w2_v)
    sq = lambda a: a.reshape(bsz, seq, wq)
    skv = lambda a: a.reshape(bsz, seq, wkv)
    o_cmp, sel = _cmp_select(sq(q), k_cmp, v_cmp)
    o_sel = _sel_attention(sq(qr), skv(ks), skv(vs), sel)
    gates = gt[:, :NSA_HEADS * 3].reshape(bsz, seq, NSA_GROUPS, NSA_REP * 3)
    gates = jnp.pad(gates, ((0, 0), (0, 0), (0, 0), (0, LANES - NSA_REP * 3))).reshape(bsz, seq, NSA_GROUPS * LANES)
    o = _win_merge(sq(qr), skv(kw), skv(vw), o_cmp, o_sel, gates)
    return _proj_ln([o.reshape(-1, wq)], [w_out.astype(BF16)], h, ln_g, ln_b)


def kernel(x, a_w_in, a_lam_q1, a_lam_k1, a_lam_q2, a_lam_k2, a_subln_g, a_w_out, c_w_in, c_pe_k, c_w1_k, c_w2_k, c_pe_v, c_w1_v, c_w2_v, c_w_out, ln_mix_g, ln_mix_b, ln_ffn_g, ln_ffn_b, router_w, router_b, w_gu, w_down, ws_gu, ws_down):
    bsz, seq, d = x.shape
    h = x.reshape(bsz * seq, d)
    depth = ln_mix_g.shape[0]
    for i in range(depth):
        j = i // 2
        if i % 2 == 0:
            h = _layer_diff_moba(h, bsz, seq, a_w_in[j], a_lam_q1[j], a_lam_k1[j], a_lam_q2[j], a_lam_k2[j],
                                 a_subln_g[j], a_w_out[j], ln_mix_g[i], ln_mix_b[i], i)
        else:
            h = _layer_nsa(h, bsz, seq, c_w_in[j], c_pe_k[j], c_w1_k[j], c_w2_k[j], c_pe_v[j], c_w1_v[j],
                           c_w2_v[j], c_w_out[j], ln_mix_g[i], ln_mix_b[i])
        h = _moe_ln(h, i, router_w[i], router_b[i], w_gu, w_down, ws_gu[i], ws_down[i],
                    ln_ffn_g[i], ln_ffn_b[i])
    return h.reshape(bsz, seq, d)
```
